```python
import math
import jax, jax.numpy as jnp
from jax import lax
import numpy as np

D_MODEL = 1024
BATCH = 16
SEQ = 2048
DEPTH = 1

CONF_DIM = D_MODEL
CONF_KERNEL = 31
SSM_HEADS = 16
SSM_HEAD_DIM = 64
SSM_DIM = SSM_HEADS * SSM_HEAD_DIM
SSM_GROUPS = 4
SSM_STATE = 128
SSM_CONV = 4
SSD_CHUNK = 128
XBC_DIM = SSM_DIM + 2 * SSM_GROUPS * SSM_STATE
D_MIX = CONF_DIM + SSM_DIM
IN_COLS = 2 * CONF_DIM + SSM_DIM + XBC_DIM + SSM_HEADS
N_EXPERT_GROUPS = 8
EXPERTS_PER_GROUP = 8
N_EXPERTS = N_EXPERT_GROUPS * EXPERTS_PER_GROUP
TOP_K = 2
D_EXPERT = D_MODEL // 2
MOE_BLOCK = 128
DEEPNORM_ALPHA = (2.0 * DEPTH) ** 0.25
DEEPNORM_BETA = (8.0 * DEPTH) ** -0.25
EPS = 1e-5

kernel_name = "hymba_conformer_ssd_hmoe_deepnorm_adaln"


def layer_norm(x, g, b):
    xf = x.astype(jnp.float32)
    mu = jnp.mean(xf, axis=-1, keepdims=True)
    var = jnp.mean(jnp.square(xf - mu), axis=-1, keepdims=True)
    return ((xf - mu) * lax.rsqrt(var + EPS)).astype(x.dtype) * g + b


def causal_dwconv(u, w, b):
    k, ch = w.shape
    y = lax.conv_general_dilated(u, w[:, None, :].astype(u.dtype), window_strides=(1,),
                                 padding=[(k - 1, 0)], dimension_numbers=("NWC", "WIO", "NWC"),
                                 feature_group_count=ch)
    return y + b


def gated_rmsnorm(y, z, w):
    g = y * jax.nn.silu(z)
    gf = g.astype(jnp.float32).reshape(g.shape[:-1] + (SSM_GROUPS, SSM_DIM // SSM_GROUPS))
    gf = gf * lax.rsqrt(jnp.mean(jnp.square(gf), axis=-1, keepdims=True) + EPS)
    return gf.reshape(g.shape).astype(y.dtype) * w


def ssd_chunked(xs, dt, A, Bm, Cm):
    b, t, g, r, p = xs.shape
    n = Bm.shape[-1]
    c, l = t // SSD_CHUNK, SSD_CHUNK
    xc = (xs * dt[..., None]).reshape(b, c, l, g, r, p)
    a = jnp.moveaxis((dt * A).reshape(b, c, l, g, r), 2, -1)
    a_cs = jnp.cumsum(a, axis=-1)
    bc = Bm.reshape(b, c, l, g, n)
    cc = Cm.reshape(b, c, l, g, n)
    causal = jnp.tril(jnp.ones((l, l), dtype=bool))
    seg = a_cs[..., :, None] - a_cs[..., None, :]
    decay = jnp.exp(jnp.where(causal, seg, -jnp.inf))
    cb = jnp.einsum("bclgn,bcsgn->bcgls", cc, bc)
    y_diag = jnp.einsum("bcgls,bcgrls,bcsgrp->bclgrp", cb, decay, xc)
    decay_to_end = jnp.exp(a_cs[..., -1:] - a_cs)
    chunk_states = jnp.einsum("bclgn,bcgrl,bclgrp->bcgrpn", bc, decay_to_end, xc)
    chunk_decay = jnp.exp(a_cs[..., -1])

    def step(state, inp):
        s_c, d_c = inp
        return d_c[..., None, None] * state + s_c, state

    init = jnp.zeros_like(chunk_states[:, 0])
    _, prev = lax.scan(step, init, (jnp.moveaxis(chunk_states, 1, 0), jnp.moveaxis(chunk_decay, 1, 0)))
    prev = jnp.moveaxis(prev, 0, 1)
    y_off = jnp.einsum("bclgn,bcgrpn,bcgrl->bclgrp", cc, prev, jnp.exp(a_cs))
    return (y_diag + y_off).reshape(b, t, g, r, p)


def hybrid_mixer(h, w_in, conf_conv_w, conf_conv_b, conf_ln_g, conf_ln_b, ssm_conv_w, ssm_conv_b,
                 ssm_dt_bias, ssm_A_log, ssm_D, ssm_norm_w, w_out):
    proj = jnp.einsum("btd,de->bte", h, w_in)
    s1 = CONF_DIM
    s2 = 2 * CONF_DIM
    s3 = s2 + SSM_DIM
    s4 = s3 + XBC_DIM
    conf_val, conf_gate, z, xbc, dt_raw = jnp.split(proj, [s1, s2, s3, s4], axis=-1)
    u = conf_val * jax.nn.sigmoid(conf_gate)
    u = jax.nn.silu(layer_norm(causal_dwconv(u, conf_conv_w, conf_conv_b), conf_ln_g, conf_ln_b))
    xbc = jax.nn.silu(causal_dwconv(xbc, ssm_conv_w, ssm_conv_b))
    xs, bm, cm = jnp.split(xbc, [SSM_DIM, SSM_DIM + SSM_GROUPS * SSM_STATE], axis=-1)
    b, t = xs.shape[:2]
    r = SSM_HEADS // SSM_GROUPS
    xs = xs.reshape(b, t, SSM_GROUPS, r, SSM_HEAD_DIM)
    bm = bm.reshape(b, t, SSM_GROUPS, SSM_STATE)
    cm = cm.reshape(b, t, SSM_GROUPS, SSM_STATE)
    dt = jax.nn.softplus(dt_raw + ssm_dt_bias).reshape(b, t, SSM_GROUPS, r)
    A = -jnp.exp(ssm_A_log).reshape(SSM_GROUPS, r)
    y = ssd_chunked(xs, dt, A, bm, cm) + ssm_D.reshape(SSM_GROUPS, r)[:, :, None] * xs
    y = gated_rmsnorm(y.reshape(b, t, SSM_DIM), z, ssm_norm_w)
    return jnp.einsum("bte,ed->btd", jnp.concatenate([u, y], axis=-1), w_out)


def hier_moe(h, wg, bg, we, be, w1, w3, w2):
    n_tok, d = h.shape
    p_group = jax.nn.softmax((h @ wg + bg).astype(jnp.float32), axis=-1)
    pg_top, g_idx = lax.top_k(p_group, 1)
    e_logits = (h @ we + be).astype(jnp.float32).reshape(n_tok, N_EXPERT_GROUPS, EXPERTS_PER_GROUP)
    sel = jnp.take_along_axis(e_logits, g_idx[:, :, None], axis=1)[:, 0]
    pe_top, e_local = lax.top_k(jax.nn.softmax(sel, axis=-1), TOP_K)
    pe_top = pe_top / jnp.sum(pe_top, axis=-1, keepdims=True)
    gate = (pg_top * pe_top).astype(h.dtype)
    e_idx = g_idx * EXPERTS_PER_GROUP + e_local
    m = n_tok * TOP_K
    e_flat = e_idx.reshape(m)
    tok_flat = jnp.repeat(jnp.arange(n_tok, dtype=jnp.int32), TOP_K)
    w_flat = gate.reshape(m)
    order = jnp.argsort(e_flat)
    e_sorted, tok_sorted, w_sorted = e_flat[order], tok_flat[order], w_flat[order]
    counts = jnp.bincount(e_flat, length=N_EXPERTS)
    starts = jnp.cumsum(counts) - counts
    padded = ((counts + MOE_BLOCK - 1) // MOE_BLOCK) * MOE_BLOCK
    pad_ends = jnp.cumsum(padded)
    pad_starts = pad_ends - padded
    dest = pad_starts[e_sorted] + (jnp.arange(m) - starts[e_sorted])
    n_slots = ((m + MOE_BLOCK - 1) // MOE_BLOCK) * MOE_BLOCK + N_EXPERTS * MOE_BLOCK
    n_blocks = n_slots // MOE_BLOCK
    slot_tok = jnp.full((n_slots,), n_tok, dtype=jnp.int32).at[dest].set(tok_sorted)
    slot_w = jnp.zeros((n_slots,), dtype=h.dtype).at[dest].set(w_sorted)
    block_exp = jnp.minimum(jnp.searchsorted(pad_ends, jnp.arange(n_blocks) * MOE_BLOCK, side="right"),
                            N_EXPERTS - 1)
    h_pad = jnp.concatenate([h, jnp.zeros((1, d), dtype=h.dtype)], axis=0)
    xs = h_pad[slot_tok].reshape(n_blocks, MOE_BLOCK, d)

    def expert_block(args):
        xb, e = args
        return (jax.nn.silu(xb @ w1[e]) * (xb @ w3[e])) @ w2[e]

    ys = lax.map(expert_block, (xs, block_exp)).reshape(n_slots, d) * slot_w[:, None]
    return jax.ops.segment_sum(ys, slot_tok, num_segments=n_tok + 1)[:n_tok]


def setup_inputs(seed: int = 0) -> dict:
    key = jax.random.key(seed)
    ks = jax.random.split(key, 28)
    f32 = jnp.float32
    L = DEPTH

    def nrm(k, shape, std):
        return jax.random.normal(k, shape, dtype=f32) * std

    xavier_out = math.sqrt(2.0 / (D_MIX + D_MODEL)) * DEEPNORM_BETA
    xavier_exp = math.sqrt(2.0 / (D_EXPERT + D_MODEL)) * DEEPNORM_BETA
    dt0 = jnp.exp(jax.random.uniform(ks[10], (L, SSM_HEADS), minval=math.log(1e-3), maxval=math.log(1e-1)))
    return {
        "x": nrm(ks[0], (BATCH, SEQ, D_MODEL), 1.0),
        "c": nrm(ks[1], (BATCH, D_MODEL), 1.0),
        "w_ada": nrm(ks[2], (L, D_MODEL, 6 * D_MODEL), D_MODEL ** -0.5),
        "b_ada": nrm(ks[3], (L, 6 * D_MODEL), 0.02),
        "w_in": nrm(ks[4], (L, D_MODEL, IN_COLS), D_MODEL ** -0.5),
        "conf_conv_w": nrm(ks[5], (L, CONF_KERNEL, CONF_DIM), CONF_KERNEL ** -0.5),
        "conf_conv_b": nrm(ks[6], (L, CONF_DIM), 0.02),
        "conf_ln_g": 1.0 + nrm(ks[7], (L, CONF_DIM), 0.02),
        "conf_ln_b": nrm(ks[8], (L, CONF_DIM), 0.02),
        "ssm_conv_w": nrm(ks[9], (L, SSM_CONV, XBC_DIM), SSM_CONV ** -0.5),
        "ssm_conv_b": nrm(ks[11], (L, XBC_DIM), 0.02),
        "ssm_dt_bias": dt0 + jnp.log(-jnp.expm1(-dt0)),
        "ssm_A_log": jnp.log(jax.random.uniform(ks[12], (L, SSM_HEADS), minval=1.0, maxval=16.0)),
        "ssm_D": 1.0 + nrm(ks[13], (L, SSM_HEADS), 0.1),
        "ssm_norm_w": 1.0 + nrm(ks[14], (L, SSM_DIM), 0.02),
        "w_out": nrm(ks[15], (L, D_MIX, D_MODEL), xavier_out),
        "ln1_g": 1.0 + nrm(ks[16], (L, D_MODEL), 0.02),
        "ln1_b": nrm(ks[17], (L, D_MODEL), 0.02),
        "router_group_w": nrm(ks[18], (L, D_MODEL, N_EXPERT_GROUPS), D_MODEL ** -0.5),
        "router_group_b": nrm(ks[19], (L, N_EXPERT_GROUPS), 0.01),
        "router_expert_w": nrm(ks[20], (L, D_MODEL, N_EXPERTS), D_MODEL ** -0.5),
        "router_expert_b": nrm(ks[21], (L, N_EXPERTS), 0.01),
        "expert_w1": nrm(ks[22], (L, N_EXPERTS, D_MODEL, D_EXPERT), D_MODEL ** -0.5),
        "expert_w3": nrm(ks[23], (L, N_EXPERTS, D_MODEL, D_EXPERT), D_MODEL ** -0.5),
        "expert_w2": nrm(ks[24], (L, N_EXPERTS, D_EXPERT, D_MODEL), xavier_exp),
        "ln2_g": 1.0 + nrm(ks[25], (L, D_MODEL), 0.02),
        "ln2_b": nrm(ks[26], (L, D_MODEL), 0.02),
    }


def reference(x, c, w_ada, b_ada, w_in, conf_conv_w, conf_conv_b, conf_ln_g, conf_ln_b,
              ssm_conv_w, ssm_conv_b, ssm_dt_bias, ssm_A_log, ssm_D, ssm_norm_w, w_out,
              ln1_g, ln1_b, router_group_w, router_group_b, router_expert_w, router_expert_b,
              expert_w1, expert_w3, expert_w2, ln2_g, ln2_b):
    cond = jax.nn.silu(c)
    for l in range(DEPTH):
        mod = cond @ w_ada[l] + b_ada[l]
        sh1, sc1, g1, sh2, sc2, g2 = [m[:, None, :] for m in jnp.split(mod, 6, axis=-1)]
        h = x * (1.0 + sc1) + sh1
        mix = hybrid_mixer(h, w_in[l], conf_conv_w[l], conf_conv_b[l], conf_ln_g[l], conf_ln_b[l],
                           ssm_conv_w[l], ssm_conv_b[l], ssm_dt_bias[l], ssm_A_log[l], ssm_D[l],
                           ssm_norm_w[l], w_out[l])
        x = layer_norm(DEEPNORM_ALPHA * x + g1 * mix, ln1_g[l], ln1_b[l])
        h = x * (1.0 + sc2) + sh2
        ffn = hier_moe(h.reshape(-1, D_MODEL), router_group_w[l], router_group_b[l],
                       router_expert_w[l], router_expert_b[l], expert_w1[l], expert_w3[l],
                       expert_w2[l]).reshape(x.shape)
        x = layer_norm(DEEPNORM_ALPHA * x + g2 * ffn, ln2_g[l], ln2_b[l])
    return x
```

```python
import functools

import jax
import jax.numpy as jnp
from jax import lax
from jax.experimental import pallas as pl
from jax.experimental.pallas import tpu as pltpu

F32 = jnp.float32
BF16 = jnp.bfloat16
I32 = jnp.int32

EPS = 1e-5
LANES = 128
SUBLANES = 8
VMEM_LIMIT = 56 * 1024 * 1024

CONF_KERNEL = 31
SSM_HEADS = 16
SSM_HEAD_DIM = 64
SSM_GROUPS = 4
SSM_STATE = 128
SSM_CONV = 4
SSD_CHUNK = 128
N_EXPERT_GROUPS = 8
EXPERTS_PER_GROUP = 8
N_EXPERTS = N_EXPERT_GROUPS * EXPERTS_PER_GROUP
DEPTH = 1
ALPHA = (2.0 * DEPTH) ** 0.25

MIX_L = 256
ROWS = 32
HIST_U = 32
HIST_X = 8
ROUTE_TB = 256
FFN_BM = 256
COMB_TC = 256


def _ln(v, g, b):
    mu = jnp.mean(v, axis=-1, keepdims=True)
    d = v - mu
    var = jnp.mean(d * d, axis=-1, keepdims=True)
    return d * lax.rsqrt(var + EPS) * g + b


def _split3(a):
    hi = a.astype(BF16)
    r1 = a - hi.astype(F32)
    mid = r1.astype(BF16)
    lo = (r1 - mid.astype(F32)).astype(BF16)
    return hi, mid, lo


def _dot(a, b):
    return jnp.dot(a, b, preferred_element_type=F32)


_NT = (((1,), (1,)), ((), ()))
_TN = (((0,), (0,)), ((), ()))


def _ada_kernel(c_ref, w_ref, b_ref, o_ref):
    c = c_ref[...]
    cond = c * jax.nn.sigmoid(c)
    c3 = _split3(cond)
    w3 = _split3(w_ref[...])
    acc = _dot(c3[0], w3[0])
    for i, j in ((0, 1), (1, 0), (0, 2), (2, 0), (1, 1)):
        acc = acc + _dot(c3[i], w3[j])
    o_ref[...] = acc + b_ref[...]


def _ada(c, w_ada, b_ada):
    bsz, d = c.shape
    n = w_ada.shape[1]
    return pl.pallas_call(
        _ada_kernel,
        grid=(n // d,),
        in_specs=[
            pl.BlockSpec((bsz, d), lambda j: (0, 0)),
            pl.BlockSpec((d, d), lambda j: (0, j)),
            pl.BlockSpec((1, d), lambda j: (0, j)),
        ],
        out_specs=pl.BlockSpec((bsz, d), lambda j: (0, j)),
        out_shape=jax.ShapeDtypeStruct((bsz, n), F32),
        compiler_params=pltpu.CompilerParams(dimension_semantics=("arbitrary",)),
        name="adaln",
    )(c, w_ada, b_ada.reshape(1, n))


def _mixer_kernel(x_ref, mod_ref, win_ref, ccw_ref, ccb_ref, clg_ref, clb_ref,
                  scw_ref, scb_ref, dtb_ref, alog_ref, dpair_ref, nw_ref, wout_ref,
                  l1g_ref, l1b_ref, o_ref,
                  h_s, cv_s, z_s, dt_s, u_s, c_s, xb_s, xc_s, y_s, mix_s, st_s):
    L = MIX_L
    d = x_ref.shape[-1]
    n_cb = d // LANES
    n_xb = xb_s.shape[0]
    n_sub = L // SSD_CHUNK

    @pl.when(pl.program_id(1) == 0)
    def _():
        u_s[:, 0:HIST_U, :] = jnp.zeros((n_cb, HIST_U, LANES), F32)
        xb_s[:, 0:HIST_X, :] = jnp.zeros((n_xb, HIST_X, LANES), F32)
        st_s[...] = jnp.zeros(st_s.shape, F32)

    sh1 = mod_ref[0, 0:1, :]
    sc1 = mod_ref[0, 1:2, :]
    g1 = mod_ref[0, 2:3, :]

    h_s[...] = (x_ref[0] * (1.0 + sc1) + sh1).astype(BF16)
    c0, c1, c2, c3 = 2 * d, 3 * d, 3 * d + n_xb * LANES, 3 * d + n_xb * LANES + LANES
    cv_s[...] = _dot(h_s[...], win_ref[:, 0:c0])
    z_s[...] = _dot(h_s[...], win_ref[:, c0:c1])
    xbc = _dot(h_s[...], win_ref[:, c1:c2])
    for cb in range(n_xb):
        xb_s[cb, HIST_X:HIST_X + L, :] = xbc[:, cb * LANES:(cb + 1) * LANES]
    dt_s[...] = _dot(h_s[...], win_ref[:, c2:c3])

    def glu(i, c):
        r0 = pl.multiple_of(i * ROWS, ROWS)
        val = cv_s[pl.ds(r0, ROWS), 0:d]
        gate = cv_s[pl.ds(r0, ROWS), d:2 * d]
        u = val * jax.nn.sigmoid(gate)
        for cb in range(n_cb):
            u_s[cb, pl.ds(pl.multiple_of(HIST_U + r0, SUBLANES), ROWS), :] = u[:, cb * LANES:(cb + 1) * LANES]
        return c
    lax.fori_loop(0, L // ROWS, glu, 0)

    win_u = SSD_CHUNK + HIST_U

    def conv_u(idx, c):
        cb = idx // n_sub
        rb = idx - cb * n_sub
        r0 = pl.multiple_of(rb * SSD_CHUNK, SSD_CHUNK)
        win = u_s[cb, pl.ds(r0, win_u), :]
        wblk = ccw_ref[cb]
        acc = jnp.broadcast_to(ccb_ref[cb], (SSD_CHUNK, LANES))
        for r in range(SUBLANES):
            sh = win if r == 0 else pltpu.roll(win, win_u - r, axis=0)
            for a in range(win_u // SUBLANES):
                k = SUBLANES * a + r - (HIST_U - (CONF_KERNEL - 1))
                if 0 <= k < CONF_KERNEL:
                    acc = acc + wblk[k:k + 1, :] * sh[SUBLANES * a:SUBLANES * a + SSD_CHUNK, :]
        c_s[cb, pl.ds(r0, SSD_CHUNK), :] = acc
        return c
    lax.fori_loop(0, n_cb * n_sub, conv_u, 0)
    for cb in range(n_cb):
        u_s[cb, 0:HIST_U, :] = u_s[cb, L:L + HIST_U, :]

    def ln_u(i, c):
        r0 = pl.multiple_of(i * ROWS, ROWS)
        v = jnp.concatenate([c_s[cb, pl.ds(r0, ROWS), :] for cb in range(n_cb)], axis=1)
        y = _ln(v, clg_ref[...], clb_ref[...])
        y = y * jax.nn.sigmoid(y)
        mix_s[pl.ds(r0, ROWS), 0:d] = y.astype(BF16)
        return c
    lax.fori_loop(0, L // ROWS, ln_u, 0)

    win_x = SSD_CHUNK + HIST_X

    def conv_x(idx, c):
        cb = idx // n_sub
        rb = idx - cb * n_sub
        r0 = pl.multiple_of(rb * SSD_CHUNK, SSD_CHUNK)
        win = xb_s[cb, pl.ds(r0, win_x), :]
        wblk = scw_ref[cb]
        acc = jnp.broadcast_to(scb_ref[cb], (SSD_CHUNK, LANES))
        for k in range(SSM_CONV):
            a, r = divmod(HIST_X - (SSM_CONV - 1) + k, SUBLANES)
            sh = win if r == 0 else pltpu.roll(win, win_x - r, axis=0)
            acc = acc + wblk[k:k + 1, :] * sh[SUBLANES * a:SUBLANES * a + SSD_CHUNK, :]
        xc_s[cb, pl.ds(r0, SSD_CHUNK), :] = acc * jax.nn.sigmoid(acc)
        return c
    lax.fori_loop(0, n_xb * n_sub, conv_x, 0)
    for cb in range(n_xb):
        xb_s[cb, 0:HIST_X, :] = xb_s[cb, L:L + HIST_X, :]

    heads_per_group = SSM_HEADS // SSM_GROUPS
    row_i = lax.broadcasted_iota(I32, (SSD_CHUNK, SSD_CHUNK), 0)
    col_i = lax.broadcasted_iota(I32, (SSD_CHUNK, SSD_CHUNK), 1)
    causal = row_i >= col_i
    tri = jnp.where(causal, 1.0, 0.0).astype(BF16)
    low_half = col_i < SSM_HEAD_DIM
    a_neg = -jnp.exp(alog_ref[...])
    b_off = n_cb
    c_off = n_cb + SSM_GROUPS

    for j in range(n_sub):
        rs = slice(j * SSD_CHUNK, (j + 1) * SSD_CHUNK)
        dtr = dt_s[rs, :] + dtb_ref[...]
        dt = jnp.maximum(dtr, 0.0) + jnp.log1p(jnp.exp(-jnp.abs(dtr)))
        a3 = _split3(dt * a_neg)
        acs = _dot(tri, a3[0]) + _dot(tri, a3[1]) + _dot(tri, a3[2])
        acs_t = acs.T
        dt_t = dt.T
        last = acs[SSD_CHUNK - 1:SSD_CHUNK, :]
        e_all = jnp.exp(acs)
        w_all = dt * jnp.exp(last - acs)
        for g in range(SSM_GROUPS):
            bg = xc_s[b_off + g, rs, :].astype(BF16)
            cg = xc_s[c_off + g, rs, :].astype(BF16)
            cbm = lax.dot_general(cg, bg, _NT, preferred_element_type=F32)
            prev = st_s[g]
            yoff = _dot(cg, prev.astype(BF16))
            for half in range(heads_per_group // 2):
                pair = (g * heads_per_group) // 2 + half
                h0 = 2 * pair
                xs_pair = xc_s[pair, rs, :]
                ypair = None
                for q in range(2):
                    hh = h0 + q
                    seg = acs[:, hh:hh + 1] - acs_t[hh:hh + 1, :]
                    dec = jnp.exp(jnp.where(causal, seg, -jnp.inf))
                    m = (cbm * dec * dt_t[hh:hh + 1, :]).astype(BF16)
                    keep = low_half if q == 0 else jnp.logical_not(low_half)
                    rhs = jnp.where(keep, xs_pair, 0.0).astype(BF16)
                    part = _dot(m, rhs)
                    ypair = part if ypair is None else ypair + part
                e_pair = jnp.where(low_half, e_all[:, h0:h0 + 1], e_all[:, h0 + 1:h0 + 2])
                w_pair = jnp.where(low_half, w_all[:, h0:h0 + 1], w_all[:, h0 + 1:h0 + 2])
                cs = slice(half * LANES, (half + 1) * LANES)
                y_s[pair, rs, :] = ypair + yoff[:, cs] * e_pair + dpair_ref[pair] * xs_pair
                new = lax.dot_general(bg, (xs_pair * w_pair).astype(BF16), _TN,
                                      preferred_element_type=F32)
                st_s[g, :, cs] = e_pair[SSD_CHUNK - 1:SSD_CHUNK, :] * prev[:, cs] + new

    grp_w = d // SSM_GROUPS

    def gate_norm(i, c):
        r0 = pl.multiple_of(i * ROWS, ROWS)
        y = jnp.concatenate([y_s[cb, pl.ds(r0, ROWS), :] for cb in range(n_cb)], axis=1)
        z = z_s[pl.ds(r0, ROWS), :]
        gg = y * (z * jax.nn.sigmoid(z))
        outs = []
        for g in range(SSM_GROUPS):
            v = gg[:, g * grp_w:(g + 1) * grp_w]
            ms = jnp.mean(v * v, axis=-1, keepdims=True)
            outs.append(v * lax.rsqrt(ms + EPS))
        gn = jnp.concatenate(outs, axis=1) * nw_ref[...]
        mix_s[pl.ds(r0, ROWS), d:2 * d] = gn.astype(BF16)
        return c
    lax.fori_loop(0, L // ROWS, gate_norm, 0)

    cv_s[:, 0:d] = _dot(mix_s[...], wout_ref[...])

    def out_ln(i, c):
        r0 = pl.multiple_of(i * ROWS, ROWS)
        r = ALPHA * x_ref[0, pl.ds(r0, ROWS), :] + g1 * cv_s[pl.ds(r0, ROWS), 0:d]
        o_ref[0, pl.ds(r0, ROWS), :] = _ln(r, l1g_ref[...], l1b_ref[...])
        return c
    lax.fori_loop(0, L // ROWS, out_ln, 0)


def _const_spec(shape):
    nd = len(shape)
    return pl.BlockSpec(shape, lambda b, t: (0,) * nd)


def _mixer(x, mod3, w_in, conf_conv_w, conf_conv_b, conf_ln_g, conf_ln_b, ssm_conv_w, ssm_conv_b,
           ssm_dt_bias, ssm_A_log, ssm_D, ssm_norm_w, w_out, ln1_g, ln1_b):
    bsz, t, d = x.shape
    L = MIX_L
    xbc_dim = ssm_conv_w.shape[1]
    n_cb = d // LANES
    n_xb = xbc_dim // LANES
    in_cols = w_in.shape[1]
    win_p = jnp.pad(w_in, ((0, 0), (0, LANES - SSM_HEADS))).astype(BF16)
    ccw = jnp.pad(conf_conv_w, ((0, HIST_U - CONF_KERNEL), (0, 0))).reshape(HIST_U, n_cb, LANES).transpose(1, 0, 2)
    ccb = conf_conv_b.reshape(n_cb, 1, LANES)
    scw = jnp.pad(ssm_conv_w, ((0, SUBLANES - SSM_CONV), (0, 0))).reshape(SUBLANES, n_xb, LANES).transpose(1, 0, 2)
    scb = ssm_conv_b.reshape(n_xb, 1, LANES)
    dtb = jnp.pad(ssm_dt_bias, (0, LANES - SSM_HEADS)).reshape(1, LANES)
    alog = jnp.pad(ssm_A_log, (0, LANES - SSM_HEADS)).reshape(1, LANES)
    dpair = jnp.repeat(ssm_D, SSM_HEAD_DIM).reshape(n_cb, 1, LANES)
    row = lambda v: v.reshape(1, -1)

    in_specs = [
        pl.BlockSpec((1, L, d), lambda b, tt: (b, tt, 0)),
        pl.BlockSpec((1, 6, d), lambda b, tt: (b, 0, 0)),
        _const_spec((d, in_cols + LANES - SSM_HEADS)),
        _const_spec((n_cb, HIST_U, LANES)),
        _const_spec((n_cb, 1, LANES)),
        _const_spec((1, d)),
        _const_spec((1, d)),
        _const_spec((n_xb, SUBLANES, LANES)),
        _const_spec((n_xb, 1, LANES)),
        _const_spec((1, LANES)),
        _const_spec((1, LANES)),
        _const_spec((n_cb, 1, LANES)),
        _const_spec((1, d)),
        _const_spec((2 * d, d)),
        _const_spec((1, d)),
        _const_spec((1, d)),
    ]
    scratch = [
        pltpu.VMEM((L, d), BF16),
        pltpu.VMEM((L, 2 * d), F32),
        pltpu.VMEM((L, d), F32),
        pltpu.VMEM((L, LANES), F32),
        pltpu.VMEM((n_cb, HIST_U + L, LANES), F32),
        pltpu.VMEM((n_cb, L, LANES), F32),
        pltpu.VMEM((n_xb, HIST_X + L, LANES), F32),
        pltpu.VMEM((n_xb, L, LANES), F32),
        pltpu.VMEM((n_cb, L, LANES), F32),
        pltpu.VMEM((L, 2 * d), BF16),
        pltpu.VMEM((SSM_GROUPS, SSM_STATE, d // SSM_GROUPS), F32),
    ]
    return pl.pallas_call(
        _mixer_kernel,
        grid=(bsz, t // L),
        in_specs=in_specs,
        out_specs=pl.BlockSpec((1, L, d), lambda b, tt: (b, tt, 0)),
        out_shape=jax.ShapeDtypeStruct((bsz, t, d), F32),
        scratch_shapes=scratch,
        compiler_params=pltpu.CompilerParams(
            dimension_semantics=("arbitrary", "arbitrary"), vmem_limit_bytes=VMEM_LIMIT),
        name="mixer",
    )(x, mod3, win_p, ccw, ccb, row(conf_ln_g), row(conf_ln_b), scw, scb, dtb, alog, dpair,
      row(ssm_norm_w), w_out.astype(BF16), row(ln1_g), row(ln1_b))


def _route_kernel(x1_ref, mod_ref, wr_ref, br_ref, h2_ref, eidx_ref, rank_ref, gcol_ref, cnt_ref,
                  carry_s):
    tb = x1_ref.shape[0]

    @pl.when(pl.program_id(0) == 0)
    def _():
        carry_s[...] = jnp.zeros(carry_s.shape, F32)

    sh2 = mod_ref[0, 3:4, :]
    sc2 = mod_ref[0, 4:5, :]
    h2 = x1_ref[...] * (1.0 + sc2) + sh2
    h2_ref[...] = h2

    w3 = _split3(wr_ref[...])
    h3 = _split3(h2)
    lg = None
    for i, j in ((0, 0), (0, 1), (1, 0), (0, 2), (2, 0), (1, 1)):
        part = lax.dot_general(w3[i], h3[j], _NT, preferred_element_type=F32)
        lg = part if lg is None else lg + part
    lg = lg + br_ref[...]

    epg = EXPERTS_PER_GROUP
    io8 = lax.broadcasted_iota(I32, (epg, tb), 0).astype(F32)

    def softmax0(v):
        e = jnp.exp(v - jnp.max(v, axis=0, keepdims=True))
        return e / jnp.sum(e, axis=0, keepdims=True)

    def top1(p):
        pm = jnp.max(p, axis=0, keepdims=True)
        idx = jnp.min(jnp.where(p == pm, io8, float(epg)), axis=0, keepdims=True)
        return pm, idx

    pg, gidx = top1(softmax0(lg[0:N_EXPERT_GROUPS, :]))
    sel = jnp.zeros((epg, tb), F32)
    for g in range(N_EXPERT_GROUPS):
        lo = N_EXPERT_GROUPS + g * epg
        sel = sel + jnp.where(gidx == float(g), lg[lo:lo + epg, :], 0.0)
    pe = softmax0(sel)
    p1, i1 = top1(pe)
    p2, i2 = top1(jnp.where(io8 == i1, -1.0, pe))
    den = p1 + p2
    gate0 = pg * (p1 / den)
    gate1 = pg * (p2 / den)
    e0 = (gidx * float(epg) + i1).astype(I32)
    e1 = (gidx * float(epg) + i2).astype(I32)

    io_e = lax.broadcasted_iota(I32, (N_EXPERTS, tb), 0)
    oh0 = jnp.where(io_e == e0, 1.0, 0.0)
    oh1 = jnp.where(io_e == e1, 1.0, 0.0)
    oh = oh0 + oh1
    rr = lax.broadcasted_iota(I32, (tb, tb), 0)
    cc = lax.broadcasted_iota(I32, (tb, tb), 1)
    before = jnp.where(rr < cc, 1.0, 0.0).astype(BF16)
    base = _dot(oh.astype(BF16), before) + carry_s[...]
    r0 = jnp.sum(oh0 * base, axis=0, keepdims=True)
    r1 = jnp.sum(oh1 * base, axis=0, keepdims=True)
    carry_s[...] = carry_s[...] + jnp.sum(oh, axis=1, keepdims=True)
    cnt_ref[...] = carry_s[...]

    eidx_ref[0:1, :] = e0
    eidx_ref[1:2, :] = e1
    rank_ref[0:1, :] = r0.astype(I32)
    rank_ref[1:2, :] = r1.astype(I32)
    io_l = lax.broadcasted_iota(I32, (LANES, tb), 0)
    gt = jnp.where(io_l == 0, gate0, jnp.where(io_l == 1, gate1, 0.0))
    gcol_ref[...] = gt.T


def _route(x1f, mod3, wr, br, tokens_per_seq):
    n, d = x1f.shape
    tb = ROUTE_TB
    per_seq = tokens_per_seq // tb
    return pl.pallas_call(
        _route_kernel,
        grid=(n // tb,),
        in_specs=[
            pl.BlockSpec((tb, d), lambda i: (i, 0)),
            pl.BlockSpec((1, 6, d), lambda i: (i // per_seq, 0, 0)),
            pl.BlockSpec((LANES, d), lambda i: (0, 0)),
            pl.BlockSpec((LANES, 1), lambda i: (0, 0)),
        ],
        out_specs=[
            pl.BlockSpec((tb, d), lambda i: (i, 0)),
            pl.BlockSpec((2, tb), lambda i: (0, i)),
            pl.BlockSpec((2, tb), lambda i: (0, i)),
            pl.BlockSpec((tb, LANES), lambda i: (i, 0)),
            pl.BlockSpec((N_EXPERTS, tb), lambda i: (0, 0)),
        ],
        out_shape=[
            jax.ShapeDtypeStruct((n, d), F32),
            jax.ShapeDtypeStruct((2, n), I32),
            jax.ShapeDtypeStruct((2, n), I32),
            jax.ShapeDtypeStruct((n, LANES), F32),
            jax.ShapeDtypeStruct((N_EXPERTS, tb), F32),
        ],
        scratch_shapes=[pltpu.VMEM((N_EXPERTS, tb), F32)],
        compiler_params=pltpu.CompilerParams(dimension_semantics=("arbitrary",)),
        name="router",
    )(x1f, mod3, wr, br)


def _ffn_kernel(bexp_ref, nused_ref, idx_cur_ref, idx_nxt_ref, h2_hbm, w1_ref, w3_ref, w2_ref,
                ys_ref, idx_s, xg_s, w1b, w3b, w2b, isem, gsem):
    i = pl.program_id(0)
    nb = pl.num_programs(0)
    nused = nused_ref[0]
    bm = ys_ref.shape[0]
    slot = lax.rem(i, 2)
    nslot = 1 - slot

    def idx_copy(src_ref, s):
        return pltpu.make_async_copy(src_ref.at[0, 0], idx_s.at[s], isem.at[s])

    def rows_start(s):
        def body(r, c):
            tok = idx_s[s, r]
            pltpu.make_async_copy(h2_hbm.at[pl.ds(tok, 1)], xg_s.at[s, pl.ds(r, 1)], gsem.at[s]).start()
            return c
        lax.fori_loop(0, bm, body, 0)

    def rows_wait(s):
        pltpu.make_async_copy(h2_hbm.at[pl.ds(0, bm)], xg_s.at[s], gsem.at[s]).wait()

    @pl.when(i == 0)
    def _():
        cp = idx_copy(idx_cur_ref, 0)
        cp.start()
        cp.wait()

        @pl.when(nused > 0)
        def _():
            rows_start(0)

    has_next = jnp.logical_and(i + 1 < nb, i + 1 < nused)

    @pl.when(has_next)
    def _():
        idx_copy(idx_nxt_ref, nslot).start()

    first_of_expert = jnp.logical_or(i == 0, bexp_ref[i] != bexp_ref[jnp.maximum(i - 1, 0)])

    @pl.when(jnp.logical_and(first_of_expert, i < nused))
    def _():
        w1b[...] = w1_ref[0].astype(BF16)
        w3b[...] = w3_ref[0].astype(BF16)
        w2b[...] = w2_ref[0].astype(BF16)

    @pl.when(has_next)
    def _():
        idx_copy(idx_nxt_ref, nslot).wait()
        rows_start(nslot)

    @pl.when(i < nused)
    def _():
        rows_wait(slot)
        x = xg_s[slot].astype(BF16)
        a = _dot(x, w1b[...])
        b = _dot(x, w3b[...])
        hm = (a * jax.nn.sigmoid(a) * b).astype(BF16)
        ys_ref[...] = _dot(hm, w2b[...])

    @pl.when(i >= nused)
    def _():
        ys_ref[...] = jnp.zeros(ys_ref.shape, F32)


def _ffn(block_exp, n_used, slot_tok3, h2, w1, w3, w2):
    n, d = h2.shape
    n_blocks, _, bm = slot_tok3.shape
    de = w1.shape[-1]
    grid_spec = pltpu.PrefetchScalarGridSpec(
        num_scalar_prefetch=2,
        grid=(n_blocks,),
        in_specs=[
            pl.BlockSpec((1, 1, bm), lambda i, be, nu: (i, 0, 0)),
            pl.BlockSpec((1, 1, bm), lambda i, be, nu: (jnp.minimum(i + 1, n_blocks - 1), 0, 0)),
            pl.BlockSpec(memory_space=pl.ANY),
            pl.BlockSpec((1, d, de), lambda i, be, nu: (be[i], 0, 0)),
            pl.BlockSpec((1, d, de), lambda i, be, nu: (be[i], 0, 0)),
            pl.BlockSpec((1, de, d), lambda i, be, nu: (be[i], 0, 0)),
        ],
        out_specs=pl.BlockSpec((bm, d), lambda i, be, nu: (i, 0)),
        scratch_shapes=[
            pltpu.SMEM((2, bm), I32),
            pltpu.VMEM((2, bm, d), F32),
            pltpu.VMEM((d, de), BF16),
            pltpu.VMEM((d, de), BF16),
            pltpu.VMEM((de, d), BF16),
            pltpu.SemaphoreType.DMA((2,)),
            pltpu.SemaphoreType.DMA((2,)),
        ],
    )
    return pl.pallas_call(
        _ffn_kernel,
        grid_spec=grid_spec,
        out_shape=jax.ShapeDtypeStruct((n_blocks * bm, d), F32),
        compiler_params=pltpu.CompilerParams(
            dimension_semantics=("arbitrary",), vmem_limit_bytes=VMEM_LIMIT),
        name="expert_ffn",
    )(block_exp, n_used, slot_tok3, slot_tok3, h2, w1, w3, w2)


def _combine_kernel(idx_cur_ref, idx_nxt_ref, x1_ref, mod_ref, gcol_ref, ys_hbm, l2g_ref, l2b_ref,
                    o_ref, idx_s, yg_s, isem, gsem):
    i = pl.program_id(0)
    nb = pl.num_programs(0)
    tc = x1_ref.shape[0]
    slot = lax.rem(i, 2)
    nslot = 1 - slot

    def idx_copy(src_ref, s):
        return pltpu.make_async_copy(src_ref.at[0, 0], idx_s.at[s], isem.at[s])

    def rows_start(s):
        def body(r, c):
            src = idx_s[s, r]
            pltpu.make_async_copy(ys_hbm.at[pl.ds(src, 1)], yg_s.at[s, pl.ds(r, 1)], gsem.at[s]).start()
            return c
        lax.fori_loop(0, 2 * tc, body, 0)

    def rows_wait(s):
        pltpu.make_async_copy(ys_hbm.at[pl.ds(0, 2 * tc)], yg_s.at[s], gsem.at[s]).wait()

    @pl.when(i == 0)
    def _():
        cp = idx_copy(idx_cur_ref, 0)
        cp.start()
        cp.wait()
        rows_start(0)

    @pl.when(i + 1 < nb)
    def _():
        cp = idx_copy(idx_nxt_ref, nslot)
        cp.start()
        cp.wait()
        rows_start(nslot)

    rows_wait(slot)
    g2 = mod_ref[0, 5:6, :]

    def body(j, c):
        r0 = pl.multiple_of(j * ROWS, ROWS)
        gc = gcol_ref[pl.ds(r0, ROWS), :]
        y0 = yg_s[slot, pl.ds(r0, ROWS), :]
        y1 = yg_s[slot, pl.ds(pl.multiple_of(tc + r0, ROWS), ROWS), :]
        ffn = gc[:, 0:1] * y0 + gc[:, 1:2] * y1
        r = ALPHA * x1_ref[pl.ds(r0, ROWS), :] + g2 * ffn
        o_ref[pl.ds(r0, ROWS), :] = _ln(r, l2g_ref[...], l2b_ref[...])
        return c
    lax.fori_loop(0, tc // ROWS, body, 0)


def _combine(dest3, x1f, mod3, gcol, ys, ln2_g, ln2_b, tokens_per_seq):
    n, d = x1f.shape
    tc = COMB_TC
    nb = n // tc
    per_seq = tokens_per_seq // tc
    return pl.pallas_call(
        _combine_kernel,
        grid=(nb,),
        in_specs=[
            pl.BlockSpec((1, 1, 2 * tc), lambda i: (i, 0, 0)),
            pl.BlockSpec((1, 1, 2 * tc), lambda i: (jnp.minimum(i + 1, nb - 1), 0, 0)),
            pl.BlockSpec((tc, d), lambda i: (i, 0)),
            pl.BlockSpec((1, 6, d), lambda i: (i // per_seq, 0, 0)),
            pl.BlockSpec((tc, LANES), lambda i: (i, 0)),
            pl.BlockSpec(memory_space=pl.ANY),
            pl.BlockSpec((1, d), lambda i: (0, 0)),
            pl.BlockSpec((1, d), lambda i: (0, 0)),
        ],
        out_specs=pl.BlockSpec((tc, d), lambda i: (i, 0)),
        out_shape=jax.ShapeDtypeStruct((n, d), F32),
        scratch_shapes=[
            pltpu.SMEM((2, 2 * tc), I32),
            pltpu.VMEM((2, 2 * tc, d), F32),
            pltpu.SemaphoreType.DMA((2,)),
            pltpu.SemaphoreType.DMA((2,)),
        ],
        compiler_params=pltpu.CompilerParams(
            dimension_semantics=("arbitrary",), vmem_limit_bytes=VMEM_LIMIT),
        name="combine",
    )(dest3, dest3, x1f, mod3, gcol, ys, ln2_g.reshape(1, d), ln2_b.reshape(1, d))


def _layer(x, mod3, l, w_in, conf_conv_w, conf_conv_b, conf_ln_g, conf_ln_b, ssm_conv_w, ssm_conv_b,
           ssm_dt_bias, ssm_A_log, ssm_D, ssm_norm_w, w_out, ln1_g, ln1_b, router_group_w,
           router_group_b, router_expert_w, router_expert_b, expert_w1, expert_w3, expert_w2,
           ln2_g, ln2_b):
    bsz, t, d = x.shape
    n = bsz * t
    x1 = _mixer(x, mod3, w_in[l], conf_conv_w[l], conf_conv_b[l], conf_ln_g[l], conf_ln_b[l],
                ssm_conv_w[l], ssm_conv_b[l], ssm_dt_bias[l], ssm_A_log[l], ssm_D[l], ssm_norm_w[l],
                w_out[l], ln1_g[l], ln1_b[l])
    x1f = x1.reshape(n, d)

    n_router = N_EXPERT_GROUPS + N_EXPERTS
    wr = jnp.concatenate([router_group_w[l], router_expert_w[l]], axis=1).T
    wr = jnp.pad(wr, ((0, LANES - n_router), (0, 0)))
    br = jnp.pad(jnp.concatenate([router_group_b[l], router_expert_b[l]]), (0, LANES - n_router))
    h2, eidx, rank, gcol, cnt = _route(x1f, mod3, wr, br.reshape(LANES, 1), t)

    bm = FFN_BM
    counts = cnt[:, 0].astype(I32)
    padded = ((counts + bm - 1) // bm) * bm
    pad_ends = jnp.cumsum(padded)
    pad_starts = pad_ends - padded
    n_slots = ((2 * n + bm - 1) // bm) * bm + N_EXPERTS * bm
    n_blocks = n_slots // bm
    dest = pad_starts[eidx] + rank
    tok = jnp.broadcast_to(jnp.arange(n, dtype=I32), (2, n))
    slot_tok = jnp.zeros((n_slots,), I32).at[dest.reshape(-1)].set(tok.reshape(-1))
    block_exp = jnp.minimum(
        jnp.searchsorted(pad_ends, jnp.arange(n_blocks, dtype=I32) * bm, side="right"),
        N_EXPERTS - 1).astype(I32)
    n_used = (pad_ends[-1:] // bm).astype(I32)

    ys = _ffn(block_exp, n_used, slot_tok.reshape(n_blocks, 1, bm), h2,
              expert_w1[l], expert_w3[l], expert_w2[l])

    tc = COMB_TC
    dest3 = dest.reshape(2, n // tc, tc).transpose(1, 0, 2).reshape(n // tc, 1, 2 * tc)
    out = _combine(dest3, x1f, mod3, gcol, ys, ln2_g[l], ln2_b[l], t)
    return out.reshape(bsz, t, d)


def kernel(x, c, w_ada, b_ada, w_in, conf_conv_w, conf_conv_b, conf_ln_g, conf_ln_b, ssm_conv_w,
           ssm_conv_b, ssm_dt_bias, ssm_A_log, ssm_D, ssm_norm_w, w_out, ln1_g, ln1_b,
           router_group_w, router_group_b, router_expert_w, router_expert_b, expert_w1, expert_w3,
           expert_w2, ln2_g, ln2_b):
    bsz, t, d = x.shape
    for l in range(w_ada.shape[0]):
        mod3 = _ada(c, w_ada[l], b_ada[l]).reshape(bsz, 6, d)
        x = _layer(x, mod3, l, w_in, conf_conv_w, conf_conv_b, conf_ln_g, conf_ln_b, ssm_conv_w,
                   ssm_conv_b, ssm_dt_bias, ssm_A_log, ssm_D, ssm_norm_w, w_out, ln1_g, ln1_b,
                   router_group_w, router_group_b, router_expert_w, router_expert_b, expert_w1,
                   expert_w3, expert_w2, ln2_g, ln2_b)
    return x
```

```python
import functools

import jax
import jax.numpy as jnp
from jax import lax
from jax.experimental import pallas as pl
from jax.experimental.pallas import tpu as pltpu

F32 = jnp.float32
BF16 = jnp.bfloat16
I32 = jnp.int32

EPS = 1e-5
LANES = 128
SUBLANES = 8
VMEM_LIMIT = 56 * 1024 * 1024

CONF_KERNEL = 31
SSM_HEADS = 16
SSM_HEAD_DIM = 64
SSM_GROUPS = 4
SSM_STATE = 128
SSM_CONV = 4
SSD_CHUNK = 128
N_EXPERT_GROUPS = 8
EXPERTS_PER_GROUP = 8
N_EXPERTS = N_EXPERT_GROUPS * EXPERTS_PER_GROUP
DEPTH = 1
ALPHA = (2.0 * DEPTH) ** 0.25

MIX_L = 256
ROWS = 32
LN_ROWS = 128
HIST_U = 32
HIST_X = 8
ROUTE_TB = 256
FFN_BM = 256
COMB_TC = 256
DMA_UNROLL = 8


def _ln(v, g, b):
    mu = jnp.mean(v, axis=-1, keepdims=True)
    d = v - mu
    var = jnp.mean(d * d, axis=-1, keepdims=True)
    return d * lax.rsqrt(var + EPS) * g + b


def _split3(a):
    hi = a.astype(BF16)
    r1 = a - hi.astype(F32)
    mid = r1.astype(BF16)
    lo = (r1 - mid.astype(F32)).astype(BF16)
    return hi, mid, lo


def _dot(a, b):
    return jnp.dot(a, b, preferred_element_type=F32)


_NT = (((1,), (1,)), ((), ()))
_TN = (((0,), (0,)), ((), ()))


def _ada_kernel(c_ref, w_ref, b_ref, o_ref):
    c = c_ref[...]
    cond = c * jax.nn.sigmoid(c)
    c3 = _split3(cond)
    w3 = _split3(w_ref[...])
    acc = _dot(c3[0], w3[0])
    for i, j in ((0, 1), (1, 0), (0, 2), (2, 0), (1, 1)):
        acc = acc + _dot(c3[i], w3[j])
    o_ref[...] = acc + b_ref[...]


def _ada(c, w_ada, b_ada):
    bsz, d = c.shape
    n = w_ada.shape[1]
    return pl.pallas_call(
        _ada_kernel,
        grid=(n // d,),
        in_specs=[
            pl.BlockSpec((bsz, d), lambda j: (0, 0)),
            pl.BlockSpec((d, d), lambda j: (0, j)),
            pl.BlockSpec((1, d), lambda j: (0, j)),
        ],
        out_specs=pl.BlockSpec((bsz, d), lambda j: (0, j)),
        out_shape=jax.ShapeDtypeStruct((bsz, n), F32),
        compiler_params=pltpu.CompilerParams(dimension_semantics=("arbitrary",)),
        name="adaln",
    )(c, w_ada, b_ada.reshape(1, n))


def _mixer_kernel(x_ref, mod_ref, win_ref, ccw_ref, ccb_ref, clg_ref, clb_ref,
                  scw_ref, scb_ref, dtb_ref, alog_ref, dpair_ref, nw_ref, wout_ref,
                  l1g_ref, l1b_ref, o_ref,
                  h_s, cv_s, z_s, dt_s, u_s, c_s, xb_s, xc_s, y_s, mix_s, st_s):
    L = MIX_L
    d = x_ref.shape[-1]
    n_cb = d // LANES
    n_xb = xb_s.shape[0]
    n_sub = L // SSD_CHUNK

    @pl.when(pl.program_id(1) == 0)
    def _():
        u_s[:, 0:HIST_U, :] = jnp.zeros((n_cb, HIST_U, LANES), F32)
        xb_s[:, 0:HIST_X, :] = jnp.zeros((n_xb, HIST_X, LANES), F32)
        st_s[...] = jnp.zeros(st_s.shape, F32)

    sh1 = mod_ref[0, 0:1, :]
    sc1 = mod_ref[0, 1:2, :]
    g1 = mod_ref[0, 2:3, :]

    h_s[...] = (x_ref[0] * (1.0 + sc1) + sh1).astype(BF16)
    c0, c1, c2, c3 = 2 * d, 3 * d, 3 * d + n_xb * LANES, 3 * d + n_xb * LANES + LANES
    cv_s[...] = _dot(h_s[...], win_ref[:, 0:c0])
    z_s[...] = _dot(h_s[...], win_ref[:, c0:c1])
    xbc = _dot(h_s[...], win_ref[:, c1:c2])
    for cb in range(n_xb):
        xb_s[cb, HIST_X:HIST_X + L, :] = xbc[:, cb * LANES:(cb + 1) * LANES]
    dt_s[...] = _dot(h_s[...], win_ref[:, c2:c3])

    def glu(i, c):
        r0 = pl.multiple_of(i * ROWS, ROWS)
        val = cv_s[pl.ds(r0, ROWS), 0:d]
        gate = cv_s[pl.ds(r0, ROWS), d:2 * d]
        u = val * jax.nn.sigmoid(gate)
        for cb in range(n_cb):
            u_s[cb, pl.ds(pl.multiple_of(HIST_U + r0, SUBLANES), ROWS), :] = u[:, cb * LANES:(cb + 1) * LANES]
        return c
    lax.fori_loop(0, L // ROWS, glu, 0)

    win_u = SSD_CHUNK + HIST_U

    def conv_u(idx, c):
        cb = idx // n_sub
        rb = idx - cb * n_sub
        r0 = pl.multiple_of(rb * SSD_CHUNK, SSD_CHUNK)
        win = u_s[cb, pl.ds(r0, win_u), :]
        wblk = ccw_ref[cb]
        acc = jnp.broadcast_to(ccb_ref[cb], (SSD_CHUNK, LANES))
        for r in range(SUBLANES):
            sh = win if r == 0 else pltpu.roll(win, win_u - r, axis=0)
            for a in range(win_u // SUBLANES):
                k = SUBLANES * a + r - (HIST_U - (CONF_KERNEL - 1))
                if 0 <= k < CONF_KERNEL:
                    acc = acc + wblk[k:k + 1, :] * sh[SUBLANES * a:SUBLANES * a + SSD_CHUNK, :]
        c_s[cb, pl.ds(r0, SSD_CHUNK), :] = acc
        return c
    lax.fori_loop(0, n_cb * n_sub, conv_u, 0)
    for cb in range(n_cb):
        u_s[cb, 0:HIST_U, :] = u_s[cb, L:L + HIST_U, :]

    def ln_u(i, c):
        r0 = pl.multiple_of(i * LN_ROWS, LN_ROWS)
        v = jnp.concatenate([c_s[cb, pl.ds(r0, LN_ROWS), :] for cb in range(n_cb)], axis=1)
        y = _ln(v, clg_ref[...], clb_ref[...])
        y = y * jax.nn.sigmoid(y)
        mix_s[pl.ds(r0, LN_ROWS), 0:d] = y.astype(BF16)
        return c
    lax.fori_loop(0, L // LN_ROWS, ln_u, 0)

    win_x = SSD_CHUNK + HIST_X

    def conv_x(idx, c):
        cb = idx // n_sub
        rb = idx - cb * n_sub
        r0 = pl.multiple_of(rb * SSD_CHUNK, SSD_CHUNK)
        win = xb_s[cb, pl.ds(r0, win_x), :]
        wblk = scw_ref[cb]
        acc = jnp.broadcast_to(scb_ref[cb], (SSD_CHUNK, LANES))
        for k in range(SSM_CONV):
            a, r = divmod(HIST_X - (SSM_CONV - 1) + k, SUBLANES)
            sh = win if r == 0 else pltpu.roll(win, win_x - r, axis=0)
            acc = acc + wblk[k:k + 1, :] * sh[SUBLANES * a:SUBLANES * a + SSD_CHUNK, :]
        xc_s[cb, pl.ds(r0, SSD_CHUNK), :] = acc * jax.nn.sigmoid(acc)
        return c
    lax.fori_loop(0, n_xb * n_sub, conv_x, 0)
    for cb in range(n_xb):
        xb_s[cb, 0:HIST_X, :] = xb_s[cb, L:L + HIST_X, :]

    heads_per_group = SSM_HEADS // SSM_GROUPS
    row_i = lax.broadcasted_iota(I32, (SSD_CHUNK, SSD_CHUNK), 0)
    col_i = lax.broadcasted_iota(I32, (SSD_CHUNK, SSD_CHUNK), 1)
    causal = row_i >= col_i
    tri = jnp.where(causal, 1.0, 0.0).astype(BF16)
    low_half = col_i < SSM_HEAD_DIM
    a_neg = -jnp.exp(alog_ref[...])
    b_off = n_cb
    c_off = n_cb + SSM_GROUPS

    for j in range(n_sub):
        rs = slice(j * SSD_CHUNK, (j + 1) * SSD_CHUNK)
        dtr = dt_s[rs, :] + dtb_ref[...]
        dt = jnp.maximum(dtr, 0.0) + jnp.log1p(jnp.exp(-jnp.abs(dtr)))
        a3 = _split3(dt * a_neg)
        acs = _dot(tri, a3[0]) + _dot(tri, a3[1]) + _dot(tri, a3[2])
        acs_t = acs.T
        dt_t = dt.T
        last = acs[SSD_CHUNK - 1:SSD_CHUNK, :]
        e_all = jnp.exp(acs)
        w_all = dt * jnp.exp(last - acs)
        for g in range(SSM_GROUPS):
            bg = xc_s[b_off + g, rs, :].astype(BF16)
            cg = xc_s[c_off + g, rs, :].astype(BF16)
            cbm = lax.dot_general(cg, bg, _NT, preferred_element_type=F32)
            prev = st_s[g]
            yoff = _dot(cg, prev.astype(BF16))
            for half in range(heads_per_group // 2):
                pair = (g * heads_per_group) // 2 + half
                h0 = 2 * pair
                xs_pair = xc_s[pair, rs, :]
                ypair = None
                for q in range(2):
                    hh = h0 + q
                    seg = acs[:, hh:hh + 1] - acs_t[hh:hh + 1, :]
                    dec = jnp.exp(jnp.where(causal, seg, -jnp.inf))
                    m = (cbm * dec * dt_t[hh:hh + 1, :]).astype(BF16)
                    keep = low_half if q == 0 else jnp.logical_not(low_half)
                    rhs = jnp.where(keep, xs_pair, 0.0).astype(BF16)
                    part = _dot(m, rhs)
                    ypair = part if ypair is None else ypair + part
                e_pair = jnp.where(low_half, e_all[:, h0:h0 + 1], e_all[:, h0 + 1:h0 + 2])
                w_pair = jnp.where(low_half, w_all[:, h0:h0 + 1], w_all[:, h0 + 1:h0 + 2])
                cs = slice(half * LANES, (half + 1) * LANES)
                y_s[pair, rs, :] = ypair + yoff[:, cs] * e_pair + dpair_ref[pair] * xs_pair
                new = lax.dot_general(bg, (xs_pair * w_pair).astype(BF16), _TN,
                                      preferred_element_type=F32)
                st_s[g, :, cs] = e_pair[SSD_CHUNK - 1:SSD_CHUNK, :] * prev[:, cs] + new

    grp_w = d // SSM_GROUPS

    def gate_norm(i, c):
        r0 = pl.multiple_of(i * LN_ROWS, LN_ROWS)
        y = jnp.concatenate([y_s[cb, pl.ds(r0, LN_ROWS), :] for cb in range(n_cb)], axis=1)
        z = z_s[pl.ds(r0, LN_ROWS), :]
        gg = y * (z * jax.nn.sigmoid(z))
        outs = []
        for g in range(SSM_GROUPS):
            v = gg[:, g * grp_w:(g + 1) * grp_w]
            ms = jnp.mean(v * v, axis=-1, keepdims=True)
            outs.append(v * lax.rsqrt(ms + EPS))
        gn = jnp.concatenate(outs, axis=1) * nw_ref[...]
        mix_s[pl.ds(r0, LN_ROWS), d:2 * d] = gn.astype(BF16)
        return c
    lax.fori_loop(0, L // LN_ROWS, gate_norm, 0)

    cv_s[:, 0:d] = _dot(mix_s[...], wout_ref[...])

    def out_ln(i, c):
        r0 = pl.multiple_of(i * LN_ROWS, LN_ROWS)
        r = ALPHA * x_ref[0, pl.ds(r0, LN_ROWS), :] + g1 * cv_s[pl.ds(r0, LN_ROWS), 0:d]
        o_ref[0, pl.ds(r0, LN_ROWS), :] = _ln(r, l1g_ref[...], l1b_ref[...])
        return c
    lax.fori_loop(0, L // LN_ROWS, out_ln, 0)


def _const_spec(shape):
    nd = len(shape)
    return pl.BlockSpec(shape, lambda b, t: (0,) * nd)


def _mixer(x, mod3, w_in, conf_conv_w, conf_conv_b, conf_ln_g, conf_ln_b, ssm_conv_w, ssm_conv_b,
           ssm_dt_bias, ssm_A_log, ssm_D, ssm_norm_w, w_out, ln1_g, ln1_b):
    bsz, t, d = x.shape
    L = MIX_L
    xbc_dim = ssm_conv_w.shape[1]
    n_cb = d // LANES
    n_xb = xbc_dim // LANES
    in_cols = w_in.shape[1]
    win_p = jnp.pad(w_in, ((0, 0), (0, LANES - SSM_HEADS))).astype(BF16)
    ccw = jnp.pad(conf_conv_w, ((0, HIST_U - CONF_KERNEL), (0, 0))).reshape(HIST_U, n_cb, LANES).transpose(1, 0, 2)
    ccb = conf_conv_b.reshape(n_cb, 1, LANES)
    scw = jnp.pad(ssm_conv_w, ((0, SUBLANES - SSM_CONV), (0, 0))).reshape(SUBLANES, n_xb, LANES).transpose(1, 0, 2)
    scb = ssm_conv_b.reshape(n_xb, 1, LANES)
    dtb = jnp.pad(ssm_dt_bias, (0, LANES - SSM_HEADS)).reshape(1, LANES)
    alog = jnp.pad(ssm_A_log, (0, LANES - SSM_HEADS)).reshape(1, LANES)
    dpair = jnp.repeat(ssm_D, SSM_HEAD_DIM).reshape(n_cb, 1, LANES)
    row = lambda v: v.reshape(1, -1)

    in_specs = [
        pl.BlockSpec((1, L, d), lambda b, tt: (b, tt, 0)),
        pl.BlockSpec((1, 6, d), lambda b, tt: (b, 0, 0)),
        _const_spec((d, in_cols + LANES - SSM_HEADS)),
        _const_spec((n_cb, HIST_U, LANES)),
        _const_spec((n_cb, 1, LANES)),
        _const_spec((1, d)),
        _const_spec((1, d)),
        _const_spec((n_xb, SUBLANES, LANES)),
        _const_spec((n_xb, 1, LANES)),
        _const_spec((1, LANES)),
        _const_spec((1, LANES)),
        _const_spec((n_cb, 1, LANES)),
        _const_spec((1, d)),
        _const_spec((2 * d, d)),
        _const_spec((1, d)),
        _const_spec((1, d)),
    ]
    scratch = [
        pltpu.VMEM((L, d), BF16),
        pltpu.VMEM((L, 2 * d), F32),
        pltpu.VMEM((L, d), F32),
        pltpu.VMEM((L, LANES), F32),
        pltpu.VMEM((n_cb, HIST_U + L, LANES), F32),
        pltpu.VMEM((n_cb, L, LANES), F32),
        pltpu.VMEM((n_xb, HIST_X + L, LANES), F32),
        pltpu.VMEM((n_xb, L, LANES), F32),
        pltpu.VMEM((n_cb, L, LANES), F32),
        pltpu.VMEM((L, 2 * d), BF16),
        pltpu.VMEM((SSM_GROUPS, SSM_STATE, d // SSM_GROUPS), F32),
    ]
    return pl.pallas_call(
        _mixer_kernel,
        grid=(bsz, t // L),
        in_specs=in_specs,
        out_specs=pl.BlockSpec((1, L, d), lambda b, tt: (b, tt, 0)),
        out_shape=jax.ShapeDtypeStruct((bsz, t, d), F32),
        scratch_shapes=scratch,
        compiler_params=pltpu.CompilerParams(
            dimension_semantics=("arbitrary", "arbitrary"), vmem_limit_bytes=VMEM_LIMIT),
        name="mixer",
    )(x, mod3, win_p, ccw, ccb, row(conf_ln_g), row(conf_ln_b), scw, scb, dtb, alog, dpair,
      row(ssm_norm_w), w_out.astype(BF16), row(ln1_g), row(ln1_b))


def _route_kernel(x1_ref, mod_ref, wr_ref, br_ref, h2_ref, eidx_ref, rank_ref, gcol_ref, cnt_ref,
                  carry_s):
    tb = x1_ref.shape[0]

    @pl.when(pl.program_id(0) == 0)
    def _():
        carry_s[...] = jnp.zeros(carry_s.shape, F32)

    sh2 = mod_ref[0, 3:4, :]
    sc2 = mod_ref[0, 4:5, :]
    h2 = x1_ref[...] * (1.0 + sc2) + sh2
    for j in range(h2_ref.shape[0]):
        h2_ref[j] = h2[:, j * LANES:(j + 1) * LANES]

    w3 = _split3(wr_ref[...])
    h3 = _split3(h2)
    lg = None
    for i, j in ((0, 0), (0, 1), (1, 0), (0, 2), (2, 0), (1, 1)):
        part = lax.dot_general(w3[i], h3[j], _NT, preferred_element_type=F32)
        lg = part if lg is None else lg + part
    lg = lg + br_ref[...]

    epg = EXPERTS_PER_GROUP
    io8 = lax.broadcasted_iota(I32, (epg, tb), 0).astype(F32)

    def softmax0(v):
        e = jnp.exp(v - jnp.max(v, axis=0, keepdims=True))
        return e / jnp.sum(e, axis=0, keepdims=True)

    def top1(p):
        pm = jnp.max(p, axis=0, keepdims=True)
        idx = jnp.min(jnp.where(p == pm, io8, float(epg)), axis=0, keepdims=True)
        return pm, idx

    pg, gidx = top1(softmax0(lg[0:N_EXPERT_GROUPS, :]))
    sel = jnp.zeros((epg, tb), F32)
    for g in range(N_EXPERT_GROUPS):
        lo = N_EXPERT_GROUPS + g * epg
        sel = sel + jnp.where(gidx == float(g), lg[lo:lo + epg, :], 0.0)
    pe = softmax0(sel)
    p1, i1 = top1(pe)
    p2, i2 = top1(jnp.where(io8 == i1, -1.0, pe))
    den = p1 + p2
    gate0 = pg * (p1 / den)
    gate1 = pg * (p2 / den)
    e0 = (gidx * float(epg) + i1).astype(I32)
    e1 = (gidx * float(epg) + i2).astype(I32)

    io_e = lax.broadcasted_iota(I32, (N_EXPERTS, tb), 0)
    oh0 = jnp.where(io_e == e0, 1.0, 0.0)
    oh1 = jnp.where(io_e == e1, 1.0, 0.0)
    oh = oh0 + oh1
    rr = lax.broadcasted_iota(I32, (tb, tb), 0)
    cc = lax.broadcasted_iota(I32, (tb, tb), 1)
    before = jnp.where(rr < cc, 1.0, 0.0).astype(BF16)
    base = _dot(oh.astype(BF16), before) + carry_s[...]
    r0 = jnp.sum(oh0 * base, axis=0, keepdims=True)
    r1 = jnp.sum(oh1 * base, axis=0, keepdims=True)
    carry_s[...] = carry_s[...] + jnp.sum(oh, axis=1, keepdims=True)
    cnt_ref[...] = carry_s[...]

    eidx_ref[0:1, :] = e0
    eidx_ref[1:2, :] = e1
    rank_ref[0:1, :] = r0.astype(I32)
    rank_ref[1:2, :] = r1.astype(I32)
    io_l = lax.broadcasted_iota(I32, (LANES, tb), 0)
    gt = jnp.where(io_l == 0, gate0, jnp.where(io_l == 1, gate1, 0.0))
    gcol_ref[...] = gt.T


def _route(x1f, mod3, wr, br, tokens_per_seq):
    n, d = x1f.shape
    tb = ROUTE_TB
    per_seq = tokens_per_seq // tb
    return pl.pallas_call(
        _route_kernel,
        grid=(n // tb,),
        in_specs=[
            pl.BlockSpec((tb, d), lambda i: (i, 0)),
            pl.BlockSpec((1, 6, d), lambda i: (i // per_seq, 0, 0)),
            pl.BlockSpec((LANES, d), lambda i: (0, 0)),
            pl.BlockSpec((LANES, 1), lambda i: (0, 0)),
        ],
        out_specs=[
            pl.BlockSpec((d // LANES, tb, LANES), lambda i: (0, i, 0)),
            pl.BlockSpec((2, tb), lambda i: (0, i)),
            pl.BlockSpec((2, tb), lambda i: (0, i)),
            pl.BlockSpec((tb, LANES), lambda i: (i, 0)),
            pl.BlockSpec((N_EXPERTS, tb), lambda i: (0, 0)),
        ],
        out_shape=[
            jax.ShapeDtypeStruct((d // LANES, n, LANES), F32),
            jax.ShapeDtypeStruct((2, n), I32),
            jax.ShapeDtypeStruct((2, n), I32),
            jax.ShapeDtypeStruct((n, LANES), F32),
            jax.ShapeDtypeStruct((N_EXPERTS, tb), F32),
        ],
        scratch_shapes=[pltpu.VMEM((N_EXPERTS, tb), F32)],
        compiler_params=pltpu.CompilerParams(dimension_semantics=("arbitrary",)),
        name="router",
    )(x1f, mod3, wr, br)


def _on_slot(slot, fn):
    for s in range(2):
        @pl.when(slot == s)
        def _():
            fn(s)


class _RowGather:
    def __init__(self, src_hbm, idx_s, buf_s, isem, gsem, n_rows):
        self.src, self.idx_s, self.buf, self.isem, self.gsem, self.n = src_hbm, idx_s, buf_s, isem, gsem, n_rows

    def _idx_copy(self, idx_vmem_ref, s):
        return pltpu.make_async_copy(idx_vmem_ref.at[0, 0], self.idx_s.at[pl.ds(s * self.n, self.n)],
                                     self.isem.at[s])

    def idx_start(self, idx_vmem_ref, s):
        self._idx_copy(idx_vmem_ref, s).start()

    def idx_wait(self, idx_vmem_ref, s):
        self._idx_copy(idx_vmem_ref, s).wait()

    def rows_start(self, s):
        def body(r8, c):
            for u in range(DMA_UNROLL):
                r = r8 * DMA_UNROLL + u
                row = self.idx_s[s * self.n + r]
                pltpu.make_async_copy(self.src.at[:, row], self.buf.at[s, :, r], self.gsem.at[s]).start()
            return c
        lax.fori_loop(0, self.n // DMA_UNROLL, body, 0)

    def rows_wait(self, s):
        pltpu.make_async_copy(self.src.at[:, pl.ds(0, self.n)], self.buf.at[s], self.gsem.at[s]).wait()


def _ffn_kernel(bexp_ref, nused_ref, idx_cur_ref, idx_nxt_ref, h2_hbm, w1_ref, w3_ref, w2_ref,
                ys_ref, idx_s, xg_s, w1b, w3b, w2b, isem, gsem):
    i = pl.program_id(0)
    nb = pl.num_programs(0)
    nused = nused_ref[0]
    n_ch, bm = ys_ref.shape[0], ys_ref.shape[1]
    slot = lax.rem(i, 2)
    nslot = 1 - slot
    gather = _RowGather(h2_hbm, idx_s, xg_s, isem, gsem, bm)

    @pl.when(i == 0)
    def _():
        gather.idx_start(idx_cur_ref, 0)
        gather.idx_wait(idx_cur_ref, 0)

        @pl.when(nused > 0)
        def _():
            gather.rows_start(0)

    has_next = jnp.logical_and(i + 1 < nb, i + 1 < nused)

    @pl.when(has_next)
    def _():
        _on_slot(nslot, functools.partial(gather.idx_start, idx_nxt_ref))

    first_of_expert = jnp.logical_or(i == 0, bexp_ref[i] != bexp_ref[jnp.maximum(i - 1, 0)])

    @pl.when(jnp.logical_and(first_of_expert, i < nused))
    def _():
        w1b[...] = w1_ref[0].astype(BF16)
        w3b[...] = w3_ref[0].astype(BF16)
        w2b[...] = w2_ref[0].astype(BF16)

    @pl.when(has_next)
    def _():
        def nxt(s):
            gather.idx_wait(idx_nxt_ref, s)
            gather.rows_start(s)
        _on_slot(nslot, nxt)

    @pl.when(i < nused)
    def _():
        _on_slot(slot, gather.rows_wait)
        x = jnp.concatenate([xg_s[slot, j] for j in range(n_ch)], axis=1).astype(BF16)
        a = _dot(x, w1b[...])
        b = _dot(x, w3b[...])
        hm = (a * jax.nn.sigmoid(a) * b).astype(BF16)
        y = _dot(hm, w2b[...])
        for j in range(n_ch):
            ys_ref[j] = y[:, j * LANES:(j + 1) * LANES]

    @pl.when(i >= nused)
    def _():
        ys_ref[...] = jnp.zeros(ys_ref.shape, F32)


def _ffn(block_exp, n_used, slot_tok3, h2, w1, w3, w2):
    n_ch, n, _ = h2.shape
    d = n_ch * LANES
    n_blocks, _, bm = slot_tok3.shape
    de = w1.shape[-1]
    grid_spec = pltpu.PrefetchScalarGridSpec(
        num_scalar_prefetch=2,
        grid=(n_blocks,),
        in_specs=[
            pl.BlockSpec((1, 1, bm), lambda i, be, nu: (i, 0, 0)),
            pl.BlockSpec((1, 1, bm), lambda i, be, nu: (jnp.minimum(i + 1, n_blocks - 1), 0, 0)),
            pl.BlockSpec(memory_space=pl.ANY),
            pl.BlockSpec((1, d, de), lambda i, be, nu: (be[i], 0, 0)),
            pl.BlockSpec((1, d, de), lambda i, be, nu: (be[i], 0, 0)),
            pl.BlockSpec((1, de, d), lambda i, be, nu: (be[i], 0, 0)),
        ],
        out_specs=pl.BlockSpec((n_ch, bm, LANES), lambda i, be, nu: (0, i, 0)),
        scratch_shapes=[
            pltpu.SMEM((2 * bm,), I32),
            pltpu.VMEM((2, n_ch, bm, LANES), F32),
            pltpu.VMEM((d, de), BF16),
            pltpu.VMEM((d, de), BF16),
            pltpu.VMEM((de, d), BF16),
            pltpu.SemaphoreType.DMA((2,)),
            pltpu.SemaphoreType.DMA((2,)),
        ],
    )
    return pl.pallas_call(
        _ffn_kernel,
        grid_spec=grid_spec,
        out_shape=jax.ShapeDtypeStruct((n_ch, n_blocks * bm, LANES), F32),
        compiler_params=pltpu.CompilerParams(
            dimension_semantics=("arbitrary",), vmem_limit_bytes=VMEM_LIMIT),
        name="expert_ffn",
    )(block_exp, n_used, slot_tok3, slot_tok3, h2, w1, w3, w2)


def _combine_kernel(idx_cur_ref, idx_nxt_ref, x1_ref, mod_ref, gcol_ref, ys_hbm, l2g_ref, l2b_ref,
                    o_ref, idx_s, yg_s, isem, gsem):
    i = pl.program_id(0)
    nb = pl.num_programs(0)
    tc = x1_ref.shape[0]
    n_ch = yg_s.shape[1]
    slot = lax.rem(i, 2)
    nslot = 1 - slot
    gather = _RowGather(ys_hbm, idx_s, yg_s, isem, gsem, 2 * tc)

    @pl.when(i == 0)
    def _():
        gather.idx_start(idx_cur_ref, 0)
        gather.idx_wait(idx_cur_ref, 0)
        gather.rows_start(0)

    @pl.when(i + 1 < nb)
    def _():
        _on_slot(nslot, functools.partial(gather.idx_start, idx_nxt_ref))

    _on_slot(slot, gather.rows_wait)

    @pl.when(i + 1 < nb)
    def _():
        def nxt(s):
            gather.idx_wait(idx_nxt_ref, s)
            gather.rows_start(s)
        _on_slot(nslot, nxt)

    g2 = mod_ref[0, 5:6, :]

    def rows_of(first, r0):
        start = pl.multiple_of(first + r0, LN_ROWS)
        return jnp.concatenate([yg_s[slot, j, pl.ds(start, LN_ROWS), :] for j in range(n_ch)], axis=1)

    def body(j, c):
        r0 = pl.multiple_of(j * LN_ROWS, LN_ROWS)
        gc = gcol_ref[pl.ds(r0, LN_ROWS), :]
        y0 = rows_of(0, r0)
        y1 = rows_of(tc, r0)
        ffn = gc[:, 0:1] * y0 + gc[:, 1:2] * y1
        r = ALPHA * x1_ref[pl.ds(r0, LN_ROWS), :] + g2 * ffn
        o_ref[pl.ds(r0, LN_ROWS), :] = _ln(r, l2g_ref[...], l2b_ref[...])
        return c
    lax.fori_loop(0, tc // LN_ROWS, body, 0)


def _combine(dest3, x1f, mod3, gcol, ys, ln2_g, ln2_b, tokens_per_seq):
    n, d = x1f.shape
    tc = COMB_TC
    nb = n // tc
    per_seq = tokens_per_seq // tc
    return pl.pallas_call(
        _combine_kernel,
        grid=(nb,),
        in_specs=[
            pl.BlockSpec((1, 1, 2 * tc), lambda i: (i, 0, 0)),
            pl.BlockSpec((1, 1, 2 * tc), lambda i: (jnp.minimum(i + 1, nb - 1), 0, 0)),
            pl.BlockSpec((tc, d), lambda i: (i, 0)),
            pl.BlockSpec((1, 6, d), lambda i: (i // per_seq, 0, 0)),
            pl.BlockSpec((tc, LANES), lambda i: (i, 0)),
            pl.BlockSpec(memory_space=pl.ANY),
            pl.BlockSpec((1, d), lambda i: (0, 0)),
            pl.BlockSpec((1, d), lambda i: (0, 0)),
        ],
        out_specs=pl.BlockSpec((tc, d), lambda i: (i, 0)),
        out_shape=jax.ShapeDtypeStruct((n, d), F32),
        scratch_shapes=[
            pltpu.SMEM((4 * tc,), I32),
            pltpu.VMEM((2, d // LANES, 2 * tc, LANES), F32),
            pltpu.SemaphoreType.DMA((2,)),
            pltpu.SemaphoreType.DMA((2,)),
        ],
        compiler_params=pltpu.CompilerParams(
            dimension_semantics=("arbitrary",), vmem_limit_bytes=VMEM_LIMIT),
        name="combine",
    )(dest3, dest3, x1f, mod3, gcol, ys, ln2_g.reshape(1, d), ln2_b.reshape(1, d))


def _layer(x, mod3, l, w_in, conf_conv_w, conf_conv_b, conf_ln_g, conf_ln_b, ssm_conv_w, ssm_conv_b,
           ssm_dt_bias, ssm_A_log, ssm_D, ssm_norm_w, w_out, ln1_g, ln1_b, router_group_w,
           router_group_b, router_expert_w, router_expert_b, expert_w1, expert_w3, expert_w2,
           ln2_g, ln2_b):
    bsz, t, d = x.shape
    n = bsz * t
    x1 = _mixer(x, mod3, w_in[l], conf_conv_w[l], conf_conv_b[l], conf_ln_g[l], conf_ln_b[l],
                ssm_conv_w[l], ssm_conv_b[l], ssm_dt_bias[l], ssm_A_log[l], ssm_D[l], ssm_norm_w[l],
                w_out[l], ln1_g[l], ln1_b[l])
    x1f = x1.reshape(n, d)

    n_router = N_EXPERT_GROUPS + N_EXPERTS
    wr = jnp.concatenate([router_group_w[l], router_expert_w[l]], axis=1).T
    wr = jnp.pad(wr, ((0, LANES - n_router), (0, 0)))
    br = jnp.pad(jnp.concatenate([router_group_b[l], router_expert_b[l]]), (0, LANES - n_router))
    h2, eidx, rank, gcol, cnt = _route(x1f, mod3, wr, br.reshape(LANES, 1), t)

    bm = FFN_BM
    counts = cnt[:, 0].astype(I32)
    padded = ((counts + bm - 1) // bm) * bm
    pad_ends = jnp.cumsum(padded)
    pad_starts = pad_ends - padded
    n_slots = ((2 * n + bm - 1) // bm) * bm + N_EXPERTS * bm
    n_blocks = n_slots // bm
    e_iota = jnp.arange(N_EXPERTS, dtype=I32)[:, None, None]
    dest = rank + jnp.sum(jnp.where(eidx[None] == e_iota, pad_starts[:, None, None], 0), axis=0)
    tok = jnp.broadcast_to(jnp.arange(n, dtype=I32), (2, n))
    slot_tok = jnp.zeros((n_slots,), I32).at[dest.reshape(-1)].set(
        tok.reshape(-1), unique_indices=True, indices_are_sorted=False, mode="promise_in_bounds")
    block_start = jnp.arange(n_blocks, dtype=I32) * bm
    block_exp = jnp.minimum(jnp.sum((pad_ends[None, :] <= block_start[:, None]).astype(I32), axis=1),
                            N_EXPERTS - 1)
    n_used = (pad_ends[-1:] // bm).astype(I32)

    ys = _ffn(block_exp, n_used, slot_tok.reshape(n_blocks, 1, bm), h2,
              expert_w1[l], expert_w3[l], expert_w2[l])

    tc = COMB_TC
    dest3 = dest.reshape(2, n // tc, tc).transpose(1, 0, 2).reshape(n // tc, 1, 2 * tc)
    out = _combine(dest3, x1f, mod3, gcol, ys, ln2_g[l], ln2_b[l], t)
    return out.reshape(bsz, t, d)


def kernel(x, c, w_ada, b_ada, w_in, conf_conv_w, conf_conv_b, conf_ln_g, conf_ln_b, ssm_conv_w,
           ssm_conv_b, ssm_dt_bias, ssm_A_log, ssm_D, ssm_norm_w, w_out, ln1_g, ln1_b,
           router_group_w, router_group_b, router_expert_w, router_expert_b, expert_w1, expert_w3,
           expert_w2, ln2_g, ln2_b):
    bsz, t, d = x.shape
    for l in range(w_ada.shape[0]):
        mod3 = _ada(c, w_ada[l], b_ada[l]).reshape(bsz, 6, d)
        x = _layer(x, mod3, l, w_in, conf_conv_w, conf_conv_b, conf_ln_g, conf_ln_b, ssm_conv_w,
                   ssm_conv_b, ssm_dt_bias, ssm_A_log, ssm_D, ssm_norm_w, w_out, ln1_g, ln1_b,
                   router_group_w, router_group_b, router_expert_w, router_expert_b, expert_w1,
                   expert_w3, expert_w2, ln2_g, ln2_b)
    return x
```

```python
import functools

import jax
import jax.numpy as jnp
from jax import lax
from jax.experimental import pallas as pl
from jax.experimental.pallas import tpu as pltpu

F32 = jnp.float32
BF16 = jnp.bfloat16
I32 = jnp.int32

EPS = 1e-5
LANES = 128
SUBLANES = 8
VMEM_LIMIT = 56 * 1024 * 1024

CONF_KERNEL = 31
SSM_HEADS = 16
SSM_HEAD_DIM = 64
SSM_GROUPS = 4
SSM_STATE = 128
SSM_CONV = 4
SSD_CHUNK = 128
N_EXPERT_GROUPS = 8
EXPERTS_PER_GROUP = 8
N_EXPERTS = N_EXPERT_GROUPS * EXPERTS_PER_GROUP
DEPTH = 1
ALPHA = (2.0 * DEPTH) ** 0.25

MIX_L = 256
ROWS = 32
LN_ROWS = 128
HIST_U = 32
HIST_X = 8
ROUTE_TB = 256
FFN_BM = 256
COMB_TC = 256
DMA_UNROLL = 8


def _ln(v, g, b):
    mu = jnp.mean(v, axis=-1, keepdims=True)
    d = v - mu
    var = jnp.mean(d * d, axis=-1, keepdims=True)
    return d * lax.rsqrt(var + EPS) * g + b


def _split3(a):
    hi = a.astype(BF16)
    r1 = a - hi.astype(F32)
    mid = r1.astype(BF16)
    lo = (r1 - mid.astype(F32)).astype(BF16)
    return hi, mid, lo


def _dot(a, b):
    return jnp.dot(a, b, preferred_element_type=F32)


def _cat(parts):
    return jnp.concatenate(parts, axis=1)


_NT = (((1,), (1,)), ((), ()))
_TN = (((0,), (0,)), ((), ()))


def _ada_kernel(c_ref, w_ref, b_ref, o_ref):
    c = c_ref[...]
    cond = c * jax.nn.sigmoid(c)
    c3 = _split3(cond)
    w3 = _split3(w_ref[...])
    acc = _dot(c3[0], w3[0])
    for i, j in ((0, 1), (1, 0), (0, 2), (2, 0), (1, 1)):
        acc = acc + _dot(c3[i], w3[j])
    o_ref[...] = acc + b_ref[...]


def _ada(c, w_ada, b_ada):
    bsz, d = c.shape
    n = w_ada.shape[1]
    return pl.pallas_call(
        _ada_kernel,
        grid=(n // d,),
        in_specs=[
            pl.BlockSpec((bsz, d), lambda j: (0, 0)),
            pl.BlockSpec((d, d), lambda j: (0, j)),
            pl.BlockSpec((1, d), lambda j: (0, j)),
        ],
        out_specs=pl.BlockSpec((bsz, d), lambda j: (0, j)),
        out_shape=jax.ShapeDtypeStruct((bsz, n), F32),
        compiler_params=pltpu.CompilerParams(dimension_semantics=("arbitrary",)),
        name="adaln",
    )(c, w_ada, b_ada.reshape(1, n))


def _mixer_kernel(x_ref, mod_ref, win_ref, ccw_ref, ccb_ref, clg_ref, clb_ref,
                  scw_ref, scb_ref, dtb_ref, alog_ref, dpair_ref, nw_ref, wout_ref,
                  l1g_ref, l1b_ref, o_ref,
                  h_s, cv_s, z_s, dt_s, u_s, c_s, xb_s, xc_s, y_s, mix_s, st_s):
    L = MIX_L
    d = x_ref.shape[-1]
    n_cb = d // LANES
    n_xb = xb_s.shape[0]
    n_sub = L // SSD_CHUNK

    @pl.when(pl.program_id(1) == 0)
    def _():
        u_s[:, 0:HIST_U, :] = jnp.zeros((n_cb, HIST_U, LANES), F32)
        xb_s[:, 0:HIST_X, :] = jnp.zeros((n_xb, HIST_X, LANES), F32)
        st_s[...] = jnp.zeros(st_s.shape, F32)

    sh1 = mod_ref[0, 0:1, :]
    sc1 = mod_ref[0, 1:2, :]
    g1 = mod_ref[0, 2:3, :]

    h_s[...] = (x_ref[0] * (1.0 + sc1) + sh1).astype(BF16)
    c0, c1, c2, c3 = 2 * d, 3 * d, 3 * d + n_xb * LANES, 3 * d + n_xb * LANES + LANES
    cv_s[...] = _dot(h_s[...], win_ref[:, 0:c0])
    z_s[...] = _dot(h_s[...], win_ref[:, c0:c1])
    xbc = _dot(h_s[...], win_ref[:, c1:c2])
    for cb in range(n_xb):
        xb_s[cb, HIST_X:HIST_X + L, :] = xbc[:, cb * LANES:(cb + 1) * LANES]
    dt_s[...] = _dot(h_s[...], win_ref[:, c2:c3])

    def glu(i, c):
        r0 = pl.multiple_of(i * ROWS, ROWS)
        val = cv_s[pl.ds(r0, ROWS), 0:d]
        gate = cv_s[pl.ds(r0, ROWS), d:2 * d]
        u = val * jax.nn.sigmoid(gate)
        for cb in range(n_cb):
            u_s[cb, pl.ds(pl.multiple_of(HIST_U + r0, SUBLANES), ROWS), :] = u[:, cb * LANES:(cb + 1) * LANES]
        return c
    lax.fori_loop(0, L // ROWS, glu, 0)

    def conv_u(idx, c):
        cb = idx // n_sub
        rb = idx - cb * n_sub
        r0 = pl.multiple_of(rb * SSD_CHUNK, SSD_CHUNK)
        wblk = ccw_ref[cb]
        acc = jnp.broadcast_to(ccb_ref[cb], (SSD_CHUNK, LANES))
        for k in range(CONF_KERNEL):
            off = HIST_U - (CONF_KERNEL - 1) + k
            acc = acc + wblk[k:k + 1, :] * u_s[cb, pl.ds(r0 + off, SSD_CHUNK), :]
        c_s[cb, pl.ds(r0, SSD_CHUNK), :] = acc
        return c
    lax.fori_loop(0, n_cb * n_sub, conv_u, 0)
    for cb in range(n_cb):
        u_s[cb, 0:HIST_U, :] = u_s[cb, L:L + HIST_U, :]

    def ln_u(i, c):
        r0 = pl.multiple_of(i * LN_ROWS, LN_ROWS)
        v = _cat([c_s[cb, pl.ds(r0, LN_ROWS), :] for cb in range(n_cb)])
        y = _ln(v, clg_ref[...], clb_ref[...])
        y = y * jax.nn.sigmoid(y)
        mix_s[pl.ds(r0, LN_ROWS), 0:d] = y.astype(BF16)
        return c
    lax.fori_loop(0, L // LN_ROWS, ln_u, 0)

    def conv_x(idx, c):
        cb = idx // n_sub
        rb = idx - cb * n_sub
        r0 = pl.multiple_of(rb * SSD_CHUNK, SSD_CHUNK)
        wblk = scw_ref[cb]
        acc = jnp.broadcast_to(scb_ref[cb], (SSD_CHUNK, LANES))
        for k in range(SSM_CONV):
            off = HIST_X - (SSM_CONV - 1) + k
            acc = acc + wblk[k:k + 1, :] * xb_s[cb, pl.ds(r0 + off, SSD_CHUNK), :]
        xc_s[cb, pl.ds(r0, SSD_CHUNK), :] = acc * jax.nn.sigmoid(acc)
        return c
    lax.fori_loop(0, n_xb * n_sub, conv_x, 0)
    for cb in range(n_xb):
        xb_s[cb, 0:HIST_X, :] = xb_s[cb, L:L + HIST_X, :]

    heads_per_group = SSM_HEADS // SSM_GROUPS
    row_i = lax.broadcasted_iota(I32, (SSD_CHUNK, SSD_CHUNK), 0)
    col_i = lax.broadcasted_iota(I32, (SSD_CHUNK, SSD_CHUNK), 1)
    causal = row_i >= col_i
    tri = jnp.where(causal, 1.0, 0.0).astype(BF16)
    low_half = col_i < SSM_HEAD_DIM
    a_neg = -jnp.exp(alog_ref[...])
    b_off = n_cb
    c_off = n_cb + SSM_GROUPS

    for j in range(n_sub):
        rs = slice(j * SSD_CHUNK, (j + 1) * SSD_CHUNK)
        dtr = dt_s[rs, :] + dtb_ref[...]
        dt = jnp.maximum(dtr, 0.0) + jnp.log1p(jnp.exp(-jnp.abs(dtr)))
        a3 = _split3(dt * a_neg)
        acs = _dot(tri, a3[0]) + _dot(tri, a3[1]) + _dot(tri, a3[2])
        acs_t = acs.T
        dt_t = dt.T
        last = acs[SSD_CHUNK - 1:SSD_CHUNK, :]
        e_all = jnp.exp(acs)
        w_all = dt * jnp.exp(last - acs)
        for g in range(SSM_GROUPS):
            bg = xc_s[b_off + g, rs, :].astype(BF16)
            cg = xc_s[c_off + g, rs, :].astype(BF16)
            cbm = lax.dot_general(cg, bg, _NT, preferred_element_type=F32)
            prev = st_s[g]
            yoff = _dot(cg, prev.astype(BF16))
            for half in range(heads_per_group // 2):
                pair = (g * heads_per_group) // 2 + half
                h0 = 2 * pair
                xs_pair = xc_s[pair, rs, :]
                ypair = None
                for q in range(2):
                    hh = h0 + q
                    seg = acs[:, hh:hh + 1] - acs_t[hh:hh + 1, :]
                    dec = jnp.exp(jnp.where(causal, seg, -jnp.inf))
                    m = (cbm * dec * dt_t[hh:hh + 1, :]).astype(BF16)
                    keep = low_half if q == 0 else jnp.logical_not(low_half)
                    rhs = jnp.where(keep, xs_pair, 0.0).astype(BF16)
                    part = _dot(m, rhs)
                    ypair = part if ypair is None else ypair + part
                e_pair = jnp.where(low_half, e_all[:, h0:h0 + 1], e_all[:, h0 + 1:h0 + 2])
                w_pair = jnp.where(low_half, w_all[:, h0:h0 + 1], w_all[:, h0 + 1:h0 + 2])
                cs = slice(half * LANES, (half + 1) * LANES)
                y_s[pair, rs, :] = ypair + yoff[:, cs] * e_pair + dpair_ref[pair] * xs_pair
                new = lax.dot_general(bg, (xs_pair * w_pair).astype(BF16), _TN,
                                      preferred_element_type=F32)
                st_s[g, :, cs] = e_pair[SSD_CHUNK - 1:SSD_CHUNK, :] * prev[:, cs] + new

    grp_w = d // SSM_GROUPS

    def gate_norm(i, c):
        r0 = pl.multiple_of(i * LN_ROWS, LN_ROWS)
        y = _cat([y_s[cb, pl.ds(r0, LN_ROWS), :] for cb in range(n_cb)])
        z = z_s[pl.ds(r0, LN_ROWS), :]
        gg = y * (z * jax.nn.sigmoid(z))
        outs = []
        for g in range(SSM_GROUPS):
            v = gg[:, g * grp_w:(g + 1) * grp_w]
            ms = jnp.mean(v * v, axis=-1, keepdims=True)
            outs.append(v * lax.rsqrt(ms + EPS))
        gn = _cat(outs) * nw_ref[...]
        mix_s[pl.ds(r0, LN_ROWS), d:2 * d] = gn.astype(BF16)
        return c
    lax.fori_loop(0, L // LN_ROWS, gate_norm, 0)

    cv_s[:, 0:d] = _dot(mix_s[...], wout_ref[...])

    def out_ln(i, c):
        r0 = pl.multiple_of(i * LN_ROWS, LN_ROWS)
        r = ALPHA * x_ref[0, pl.ds(r0, LN_ROWS), :] + g1 * cv_s[pl.ds(r0, LN_ROWS), 0:d]
        o_ref[0, pl.ds(r0, LN_ROWS), :] = _ln(r, l1g_ref[...], l1b_ref[...])
        return c
    lax.fori_loop(0, L // LN_ROWS, out_ln, 0)


def _const_spec(shape):
    nd = len(shape)
    return pl.BlockSpec(shape, lambda b, t: (0,) * nd)


def _mixer(x, mod3, w_in, conf_conv_w, conf_conv_b, conf_ln_g, conf_ln_b, ssm_conv_w, ssm_conv_b,
           ssm_dt_bias, ssm_A_log, ssm_D, ssm_norm_w, w_out, ln1_g, ln1_b):
    bsz, t, d = x.shape
    L = MIX_L
    xbc_dim = ssm_conv_w.shape[1]
    n_cb = d // LANES
    n_xb = xbc_dim // LANES
    in_cols = w_in.shape[1]
    win_p = jnp.pad(w_in, ((0, 0), (0, LANES - SSM_HEADS))).astype(BF16)
    ccw = jnp.pad(conf_conv_w, ((0, HIST_U - CONF_KERNEL), (0, 0))).reshape(HIST_U, n_cb, LANES).transpose(1, 0, 2)
    ccb = conf_conv_b.reshape(n_cb, 1, LANES)
    scw = jnp.pad(ssm_conv_w, ((0, SUBLANES - SSM_CONV), (0, 0))).reshape(SUBLANES, n_xb, LANES).transpose(1, 0, 2)
    scb = ssm_conv_b.reshape(n_xb, 1, LANES)
    dtb = jnp.pad(ssm_dt_bias, (0, LANES - SSM_HEADS)).reshape(1, LANES)
    alog = jnp.pad(ssm_A_log, (0, LANES - SSM_HEADS)).reshape(1, LANES)
    dpair = jnp.repeat(ssm_D, SSM_HEAD_DIM).reshape(n_cb, 1, LANES)
    row = lambda v: v.reshape(1, -1)

    in_specs = [
        pl.BlockSpec((1, L, d), lambda b, tt: (b, tt, 0)),
        pl.BlockSpec((1, 6, d), lambda b, tt: (b, 0, 0)),
        _const_spec((d, in_cols + LANES - SSM_HEADS)),
        _const_spec((n_cb, HIST_U, LANES)),
        _const_spec((n_cb, 1, LANES)),
        _const_spec((1, d)),
        _const_spec((1, d)),
        _const_spec((n_xb, SUBLANES, LANES)),
        _const_spec((n_xb, 1, LANES)),
        _const_spec((1, LANES)),
        _const_spec((1, LANES)),
        _const_spec((n_cb, 1, LANES)),
        _const_spec((1, d)),
        _const_spec((2 * d, d)),
        _const_spec((1, d)),
        _const_spec((1, d)),
    ]
    scratch = [
        pltpu.VMEM((L, d), BF16),
        pltpu.VMEM((L, 2 * d), F32),
        pltpu.VMEM((L, d), F32),
        pltpu.VMEM((L, LANES), F32),
        pltpu.VMEM((n_cb, HIST_U + L, LANES), F32),
        pltpu.VMEM((n_cb, L, LANES), F32),
        pltpu.VMEM((n_xb, HIST_X + L, LANES), F32),
        pltpu.VMEM((n_xb, L, LANES), F32),
        pltpu.VMEM((n_cb, L, LANES), F32),
        pltpu.VMEM((L, 2 * d), BF16),
        pltpu.VMEM((SSM_GROUPS, SSM_STATE, d // SSM_GROUPS), F32),
    ]
    return pl.pallas_call(
        _mixer_kernel,
        grid=(bsz, t // L),
        in_specs=in_specs,
        out_specs=pl.BlockSpec((1, L, d), lambda b, tt: (b, tt, 0)),
        out_shape=jax.ShapeDtypeStruct((bsz, t, d), F32),
        scratch_shapes=scratch,
        compiler_params=pltpu.CompilerParams(
            dimension_semantics=("arbitrary", "arbitrary"), vmem_limit_bytes=VMEM_LIMIT),
        name="mixer",
    )(x, mod3, win_p, ccw, ccb, row(conf_ln_g), row(conf_ln_b), scw, scb, dtb, alog, dpair,
      row(ssm_norm_w), w_out.astype(BF16), row(ln1_g), row(ln1_b))


def _route_kernel(x1_ref, mod_ref, wr_ref, br_ref, h2_ref, eidx_ref, rank_ref, gcol_ref, cnt_ref,
                  carry_s):
    tb = x1_ref.shape[0]

    @pl.when(pl.program_id(0) == 0)
    def _():
        carry_s[...] = jnp.zeros(carry_s.shape, F32)

    sh2 = mod_ref[0, 3:4, :]
    sc2 = mod_ref[0, 4:5, :]
    h2 = x1_ref[...] * (1.0 + sc2) + sh2
    for j in range(h2_ref.shape[0]):
        h2_ref[j] = h2[:, j * LANES:(j + 1) * LANES]

    w3 = _split3(wr_ref[...])
    h3 = _split3(h2)
    lg = None
    for i, j in ((0, 0), (0, 1), (1, 0), (0, 2), (2, 0), (1, 1)):
        part = lax.dot_general(w3[i], h3[j], _NT, preferred_element_type=F32)
        lg = part if lg is None else lg + part
    lg = lg + br_ref[...]

    epg = EXPERTS_PER_GROUP
    io8 = lax.broadcasted_iota(I32, (epg, tb), 0).astype(F32)

    def softmax0(v):
        e = jnp.exp(v - jnp.max(v, axis=0, keepdims=True))
        return e / jnp.sum(e, axis=0, keepdims=True)

    def top1(p):
        pm = jnp.max(p, axis=0, keepdims=True)
        idx = jnp.min(jnp.where(p == pm, io8, float(epg)), axis=0, keepdims=True)
        return pm, idx

    pg, gidx = top1(softmax0(lg[0:N_EXPERT_GROUPS, :]))
    sel = jnp.zeros((epg, tb), F32)
    for g in range(N_EXPERT_GROUPS):
        lo = N_EXPERT_GROUPS + g * epg
        sel = sel + jnp.where(gidx == float(g), lg[lo:lo + epg, :], 0.0)
    pe = softmax0(sel)
    p1, i1 = top1(pe)
    p2, i2 = top1(jnp.where(io8 == i1, -1.0, pe))
    den = p1 + p2
    gate0 = pg * (p1 / den)
    gate1 = pg * (p2 / den)
    e0 = (gidx * float(epg) + i1).astype(I32)
    e1 = (gidx * float(epg) + i2).astype(I32)

    io_e = lax.broadcasted_iota(I32, (N_EXPERTS, tb), 0)
    oh0 = jnp.where(io_e == e0, 1.0, 0.0)
    oh1 = jnp.where(io_e == e1, 1.0, 0.0)
    oh = oh0 + oh1
    rr = lax.broadcasted_iota(I32, (tb, tb), 0)
    cc = lax.broadcasted_iota(I32, (tb, tb), 1)
    before = jnp.where(rr < cc, 1.0, 0.0).astype(BF16)
    base = _dot(oh.astype(BF16), before) + carry_s[...]
    r0 = jnp.sum(oh0 * base, axis=0, keepdims=True)
    r1 = jnp.sum(oh1 * base, axis=0, keepdims=True)
    carry_s[...] = carry_s[...] + jnp.sum(oh, axis=1, keepdims=True)
    cnt_ref[...] = carry_s[...]

    eidx_ref[0:1, :] = e0
    eidx_ref[1:2, :] = e1
    rank_ref[0:1, :] = r0.astype(I32)
    rank_ref[1:2, :] = r1.astype(I32)
    io_l = lax.broadcasted_iota(I32, (LANES, tb), 0)
    gt = jnp.where(io_l == 0, gate0, jnp.where(io_l == 1, gate1, 0.0))
    gcol_ref[...] = gt.T


def _route(x1f, mod3, wr, br, tokens_per_seq):
    n, d = x1f.shape
    tb = ROUTE_TB
    per_seq = tokens_per_seq // tb
    return pl.pallas_call(
        _route_kernel,
        grid=(n // tb,),
        in_specs=[
            pl.BlockSpec((tb, d), lambda i: (i, 0)),
            pl.BlockSpec((1, 6, d), lambda i: (i // per_seq, 0, 0)),
            pl.BlockSpec((LANES, d), lambda i: (0, 0)),
            pl.BlockSpec((LANES, 1), lambda i: (0, 0)),
        ],
        out_specs=[
            pl.BlockSpec((d // LANES, tb, LANES), lambda i: (0, i, 0)),
            pl.BlockSpec((2, tb), lambda i: (0, i)),
            pl.BlockSpec((2, tb), lambda i: (0, i)),
            pl.BlockSpec((tb, LANES), lambda i: (i, 0)),
            pl.BlockSpec((N_EXPERTS, tb), lambda i: (0, 0)),
        ],
        out_shape=[
            jax.ShapeDtypeStruct((d // LANES, n, LANES), F32),
            jax.ShapeDtypeStruct((2, n), I32),
            jax.ShapeDtypeStruct((2, n), I32),
            jax.ShapeDtypeStruct((n, LANES), F32),
            jax.ShapeDtypeStruct((N_EXPERTS, tb), F32),
        ],
        scratch_shapes=[pltpu.VMEM((N_EXPERTS, tb), F32)],
        compiler_params=pltpu.CompilerParams(dimension_semantics=("arbitrary",)),
        name="router",
    )(x1f, mod3, wr, br)


def _on_slot(slot, fn):
    for s in range(2):
        @pl.when(slot == s)
        def _():
            fn(s)


class _RowGather:
    def __init__(self, src_hbm, idx_s, buf_s, isem, gsem, n_rows, priorities):
        self.src, self.idx_s, self.buf, self.isem, self.gsem, self.n = src_hbm, idx_s, buf_s, isem, gsem, n_rows
        self.priorities = priorities

    def _idx_copy(self, idx_vmem_ref, s):
        return pltpu.make_async_copy(idx_vmem_ref.at[0, 0], self.idx_s.at[pl.ds(s * self.n, self.n)],
                                     self.isem.at[s])

    def idx_start(self, idx_vmem_ref, s):
        self._idx_copy(idx_vmem_ref, s).start()

    def idx_wait(self, idx_vmem_ref, s):
        self._idx_copy(idx_vmem_ref, s).wait()

    def rows_start(self, s):
        def body(r8, c):
            for u in range(DMA_UNROLL):
                r = r8 * DMA_UNROLL + u
                row = self.idx_s[s * self.n + r]
                pltpu.make_async_copy(self.src.at[:, row], self.buf.at[s, :, r], self.gsem.at[s]).start(
                    priority=self.priorities[u % len(self.priorities)])
            return c
        lax.fori_loop(0, self.n // DMA_UNROLL, body, 0)

    def rows_wait(self, s):
        pltpu.make_async_copy(self.src.at[:, pl.ds(0, self.n)], self.buf.at[s], self.gsem.at[s]).wait()


def _ffn_kernel(bexp_ref, nexp_ref, nused_ref, idx_cur_ref, idx_nxt_ref, h2_hbm, w1_hbm, w3_hbm, w2_hbm,
                ys_ref, idx_s, run_s, xg_s, w1f, w3f, w2f, w1b, w3b, w2b, isem, gsem, wsem):
    i = pl.program_id(0)
    nb = pl.num_programs(0)
    nused = nused_ref[0]
    n_ch, bm = ys_ref.shape[0], ys_ref.shape[1]
    slot = lax.rem(i, 2)
    nslot = 1 - slot
    gather = _RowGather(h2_hbm, idx_s, xg_s, isem, gsem, bm, priorities=(1,))

    def weight_copies(e, s):
        return [pltpu.make_async_copy(src.at[e], dst.at[s], wsem.at[s])
                for src, dst in ((w1_hbm, w1f), (w3_hbm, w3f), (w2_hbm, w2f))]

    @pl.when(i == 0)
    def _():
        run_s[0] = 0
        gather.idx_start(idx_cur_ref, 0)
        gather.idx_wait(idx_cur_ref, 0)

        @pl.when(nused > 0)
        def _():
            for cp in weight_copies(bexp_ref[0], 0):
                cp.start()
            gather.rows_start(0)

    has_next = jnp.logical_and(i + 1 < nb, i + 1 < nused)

    @pl.when(has_next)
    def _():
        _on_slot(nslot, functools.partial(gather.idx_start, idx_nxt_ref))

    first_of_run = jnp.logical_or(i == 0, bexp_ref[i] != bexp_ref[jnp.maximum(i - 1, 0)])

    @pl.when(jnp.logical_and(first_of_run, i < nused))
    def _():
        e = bexp_ref[i]
        ne = nexp_ref[i]
        run = run_s[0]

        def load(ws):
            for cp in weight_copies(e, ws):
                cp.wait()

            @pl.when(ne != e)
            def _():
                for cp in weight_copies(ne, 1 - ws):
                    cp.start()
            w1b[...] = w1f[ws].astype(BF16)
            w3b[...] = w3f[ws].astype(BF16)
            w2b[...] = w2f[ws].astype(BF16)
        _on_slot(lax.rem(run, 2), load)
        run_s[0] = run + 1

    @pl.when(has_next)
    def _():
        def nxt(s):
            gather.idx_wait(idx_nxt_ref, s)
            gather.rows_start(s)
        _on_slot(nslot, nxt)

    @pl.when(i < nused)
    def _():
        _on_slot(slot, gather.rows_wait)
        x = _cat([xg_s[slot, j] for j in range(n_ch)]).astype(BF16)
        a = _dot(x, w1b[...])
        b = _dot(x, w3b[...])
        hm = (a * jax.nn.sigmoid(a) * b).astype(BF16)
        y = _dot(hm, w2b[...])
        for j in range(n_ch):
            ys_ref[j] = y[:, j * LANES:(j + 1) * LANES]

    @pl.when(i >= nused)
    def _():
        ys_ref[...] = jnp.zeros(ys_ref.shape, F32)


def _ffn(block_exp, next_exp, n_used, slot_tok3, h2, w1, w3, w2):
    n_ch, n, _ = h2.shape
    d = n_ch * LANES
    n_blocks, _, bm = slot_tok3.shape
    de = w1.shape[-1]
    grid_spec = pltpu.PrefetchScalarGridSpec(
        num_scalar_prefetch=3,
        grid=(n_blocks,),
        in_specs=[
            pl.BlockSpec((1, 1, bm), lambda i, *_: (i, 0, 0)),
            pl.BlockSpec((1, 1, bm), lambda i, *_: (jnp.minimum(i + 1, n_blocks - 1), 0, 0)),
            pl.BlockSpec(memory_space=pl.ANY),
            pl.BlockSpec(memory_space=pl.ANY),
            pl.BlockSpec(memory_space=pl.ANY),
            pl.BlockSpec(memory_space=pl.ANY),
        ],
        out_specs=pl.BlockSpec((n_ch, bm, LANES), lambda i, *_: (0, i, 0)),
        scratch_shapes=[
            pltpu.SMEM((2 * bm,), I32),
            pltpu.SMEM((1,), I32),
            pltpu.VMEM((2, n_ch, bm, LANES), F32),
            pltpu.VMEM((2, d, de), F32),
            pltpu.VMEM((2, d, de), F32),
            pltpu.VMEM((2, de, d), F32),
            pltpu.VMEM((d, de), BF16),
            pltpu.VMEM((d, de), BF16),
            pltpu.VMEM((de, d), BF16),
            pltpu.SemaphoreType.DMA((2,)),
            pltpu.SemaphoreType.DMA((2,)),
            pltpu.SemaphoreType.DMA((2,)),
        ],
    )
    return pl.pallas_call(
        _ffn_kernel,
        grid_spec=grid_spec,
        out_shape=jax.ShapeDtypeStruct((n_ch, n_blocks * bm, LANES), F32),
        compiler_params=pltpu.CompilerParams(
            dimension_semantics=("arbitrary",), vmem_limit_bytes=VMEM_LIMIT),
        name="expert_ffn",
    )(block_exp, next_exp, n_used, slot_tok3, slot_tok3, h2, w1, w3, w2)


def _combine_kernel(idx_cur_ref, idx_nxt_ref, x1_ref, mod_ref, gcol_ref, ys_hbm, l2g_ref, l2b_ref,
                    o_ref, idx_s, yg_s, isem, gsem):
    i = pl.program_id(0)
    nb = pl.num_programs(0)
    tc = x1_ref.shape[0]
    n_ch = yg_s.shape[1]
    slot = lax.rem(i, 2)
    nslot = 1 - slot
    gather = _RowGather(ys_hbm, idx_s, yg_s, isem, gsem, 2 * tc, priorities=(0, 1))

    @pl.when(i == 0)
    def _():
        gather.idx_start(idx_cur_ref, 0)
        gather.idx_wait(idx_cur_ref, 0)
        gather.rows_start(0)

    @pl.when(i + 1 < nb)
    def _():
        _on_slot(nslot, functools.partial(gather.idx_start, idx_nxt_ref))

    _on_slot(slot, gather.rows_wait)

    @pl.when(i + 1 < nb)
    def _():
        def nxt(s):
            gather.idx_wait(idx_nxt_ref, s)
            gather.rows_start(s)
        _on_slot(nslot, nxt)

    g2 = mod_ref[0, 5:6, :]

    def rows_of(first, r0):
        start = pl.multiple_of(first + r0, LN_ROWS)
        return _cat([yg_s[slot, j, pl.ds(start, LN_ROWS), :] for j in range(n_ch)])

    def body(j, c):
        r0 = pl.multiple_of(j * LN_ROWS, LN_ROWS)
        gc = gcol_ref[pl.ds(r0, LN_ROWS), :]
        y0 = rows_of(0, r0)
        y1 = rows_of(tc, r0)
        ffn = gc[:, 0:1] * y0 + gc[:, 1:2] * y1
        r = ALPHA * x1_ref[pl.ds(r0, LN_ROWS), :] + g2 * ffn
        o_ref[pl.ds(r0, LN_ROWS), :] = _ln(r, l2g_ref[...], l2b_ref[...])
        return c
    lax.fori_loop(0, tc // LN_ROWS, body, 0)


def _combine(dest3, x1f, mod3, gcol, ys, ln2_g, ln2_b, tokens_per_seq):
    n, d = x1f.shape
    tc = COMB_TC
    nb = n // tc
    per_seq = tokens_per_seq // tc
    return pl.pallas_call(
        _combine_kernel,
        grid=(nb,),
        in_specs=[
            pl.BlockSpec((1, 1, 2 * tc), lambda i: (i, 0, 0)),
            pl.BlockSpec((1, 1, 2 * tc), lambda i: (jnp.minimum(i + 1, nb - 1), 0, 0)),
            pl.BlockSpec((tc, d), lambda i: (i, 0)),
            pl.BlockSpec((1, 6, d), lambda i: (i // per_seq, 0, 0)),
            pl.BlockSpec((tc, LANES), lambda i: (i, 0)),
            pl.BlockSpec(memory_space=pl.ANY),
            pl.BlockSpec((1, d), lambda i: (0, 0)),
            pl.BlockSpec((1, d), lambda i: (0, 0)),
        ],
        out_specs=pl.BlockSpec((tc, d), lambda i: (i, 0)),
        out_shape=jax.ShapeDtypeStruct((n, d), F32),
        scratch_shapes=[
            pltpu.SMEM((4 * tc,), I32),
            pltpu.VMEM((2, d // LANES, 2 * tc, LANES), F32),
            pltpu.SemaphoreType.DMA((2,)),
            pltpu.SemaphoreType.DMA((2,)),
        ],
        compiler_params=pltpu.CompilerParams(
            dimension_semantics=("arbitrary",), vmem_limit_bytes=VMEM_LIMIT),
        name="combine",
    )(dest3, dest3, x1f, mod3, gcol, ys, ln2_g.reshape(1, d), ln2_b.reshape(1, d))


def _layer(x, mod3, l, w_in, conf_conv_w, conf_conv_b, conf_ln_g, conf_ln_b, ssm_conv_w, ssm_conv_b,
           ssm_dt_bias, ssm_A_log, ssm_D, ssm_norm_w, w_out, ln1_g, ln1_b, router_group_w,
           router_group_b, router_expert_w, router_expert_b, expert_w1, expert_w3, expert_w2,
           ln2_g, ln2_b):
    bsz, t, d = x.shape
    n = bsz * t
    x1 = _mixer(x, mod3, w_in[l], conf_conv_w[l], conf_conv_b[l], conf_ln_g[l], conf_ln_b[l],
                ssm_conv_w[l], ssm_conv_b[l], ssm_dt_bias[l], ssm_A_log[l], ssm_D[l], ssm_norm_w[l],
                w_out[l], ln1_g[l], ln1_b[l])
    x1f = x1.reshape(n, d)

    n_router = N_EXPERT_GROUPS + N_EXPERTS
    wr = jnp.concatenate([router_group_w[l], router_expert_w[l]], axis=1).T
    wr = jnp.pad(wr, ((0, LANES - n_router), (0, 0)))
    br = jnp.pad(jnp.concatenate([router_group_b[l], router_expert_b[l]]), (0, LANES - n_router))
    h2, eidx, rank, gcol, cnt = _route(x1f, mod3, wr, br.reshape(LANES, 1), t)

    bm = FFN_BM
    e_ids = jnp.arange(N_EXPERTS, dtype=I32)
    counts = cnt[:, 0].astype(I32)
    padded = ((counts + bm - 1) // bm) * bm
    pad_ends = jnp.cumsum(padded)
    pad_starts = pad_ends - padded
    n_slots = ((2 * n + bm - 1) // bm) * bm + N_EXPERTS * bm
    n_blocks = n_slots // bm
    dest = rank + jnp.sum(jnp.where(eidx[None] == e_ids[:, None, None], pad_starts[:, None, None], 0), axis=0)
    tok = jnp.broadcast_to(jnp.arange(n, dtype=I32), (2, n))
    slot_tok = jnp.zeros((n_slots,), I32).at[dest.reshape(-1)].set(
        tok.reshape(-1), unique_indices=True, indices_are_sorted=False, mode="promise_in_bounds")
    block_start = jnp.arange(n_blocks, dtype=I32) * bm
    block_exp = jnp.minimum(jnp.sum((pad_ends[None, :] <= block_start[:, None]).astype(I32), axis=1),
                            N_EXPERTS - 1)
    n_used = (pad_ends[-1:] // bm).astype(I32)
    later = jnp.where((padded > 0)[None, :] & (e_ids[None, :] > e_ids[:, None]), e_ids[None, :], N_EXPERTS)
    nxt = jnp.min(later, axis=1)
    next_of_expert = jnp.where(nxt < N_EXPERTS, nxt, e_ids)
    next_exp = jnp.sum(jnp.where(block_exp[:, None] == e_ids[None, :], next_of_expert[None, :], 0), axis=1)

    ys = _ffn(block_exp, next_exp.astype(I32), n_used, slot_tok.reshape(n_blocks, 1, bm), h2,
              expert_w1[l], expert_w3[l], expert_w2[l])

    tc = COMB_TC
    dest3 = dest.reshape(2, n // tc, tc).transpose(1, 0, 2).reshape(n // tc, 1, 2 * tc)
    out = _combine(dest3, x1f, mod3, gcol, ys, ln2_g[l], ln2_b[l], t)
    return out.reshape(bsz, t, d)


def kernel(x, c, w_ada, b_ada, w_in, conf_conv_w, conf_conv_b, conf_ln_g, conf_ln_b, ssm_conv_w,
           ssm_conv_b, ssm_dt_bias, ssm_A_log, ssm_D, ssm_norm_w, w_out, ln1_g, ln1_b,
           router_group_w, router_group_b, router_expert_w, router_expert_b, expert_w1, expert_w3,
           expert_w2, ln2_g, ln2_b):
    bsz, t, d = x.shape
    for l in range(w_ada.shape[0]):
        mod3 = _ada(c, w_ada[l], b_ada[l]).reshape(bsz, 6, d)
        x = _layer(x, mod3, l, w_in, conf_conv_w, conf_conv_b, conf_ln_g, conf_ln_b, ssm_conv_w,
                   ssm_conv_b, ssm_dt_bias, ssm_A_log, ssm_D, ssm_norm_w, w_out, ln1_g, ln1_b,
                   router_group_w, router_group_b, router_expert_w, router_expert_b, expert_w1,
                   expert_w3, expert_w2, ln2_g, ln2_b)
    return x
```

```python
import functools

import jax
import jax.numpy as jnp
from jax import lax
from jax.experimental import pallas as pl
from jax.experimental.pallas import tpu as pltpu

F32 = jnp.float32
BF16 = jnp.bfloat16
I32 = jnp.int32

EPS = 1e-5
LANES = 128
SUBLANES = 8
VMEM_LIMIT = 56 * 1024 * 1024

CONF_KERNEL = 31
SSM_HEADS = 16
SSM_HEAD_DIM = 64
SSM_GROUPS = 4
SSM_STATE = 128
SSM_CONV = 4
SSD_CHUNK = 128
N_EXPERT_GROUPS = 8
EXPERTS_PER_GROUP = 8
N_EXPERTS = N_EXPERT_GROUPS * EXPERTS_PER_GROUP
DEPTH = 1
ALPHA = (2.0 * DEPTH) ** 0.25

MIX_L = 256
ROWS = 32
LN_ROWS = 128
HIST_U = 32
HIST_X = 8
ROUTE_TB = 256
FFN_BM = 256
COMB_TC = 256
DMA_UNROLL = 8


def _ln(v, g, b):
    mu = jnp.mean(v, axis=-1, keepdims=True)
    d = v - mu
    var = jnp.mean(d * d, axis=-1, keepdims=True)
    return d * lax.rsqrt(var + EPS) * g + b


def _split3(a):
    hi = a.astype(BF16)
    r1 = a - hi.astype(F32)
    mid = r1.astype(BF16)
    lo = (r1 - mid.astype(F32)).astype(BF16)
    return hi, mid, lo


def _dot(a, b):
    return jnp.dot(a, b, preferred_element_type=F32)


def _cat(parts):
    return jnp.concatenate(parts, axis=1)


_NT = (((1,), (1,)), ((), ()))
_TN = (((0,), (0,)), ((), ()))


def _ada_kernel(c_ref, w_ref, b_ref, o_ref):
    c = c_ref[...]
    cond = c * jax.nn.sigmoid(c)
    c3 = _split3(cond)
    w3 = _split3(w_ref[...])
    acc = _dot(c3[0], w3[0])
    for i, j in ((0, 1), (1, 0), (0, 2), (2, 0), (1, 1)):
        acc = acc + _dot(c3[i], w3[j])
    o_ref[...] = acc + b_ref[...]


def _ada(c, w_ada, b_ada):
    bsz, d = c.shape
    n = w_ada.shape[1]
    return pl.pallas_call(
        _ada_kernel,
        grid=(n // d,),
        in_specs=[
            pl.BlockSpec((bsz, d), lambda j: (0, 0)),
            pl.BlockSpec((d, d), lambda j: (0, j)),
            pl.BlockSpec((1, d), lambda j: (0, j)),
        ],
        out_specs=pl.BlockSpec((bsz, d), lambda j: (0, j)),
        out_shape=jax.ShapeDtypeStruct((bsz, n), F32),
        compiler_params=pltpu.CompilerParams(dimension_semantics=("arbitrary",)),
        name="adaln",
    )(c, w_ada, b_ada.reshape(1, n))


def _mixer_kernel(x_ref, mod_ref, win_ref, ccw_ref, ccb_ref, clg_ref, clb_ref,
                  scw_ref, scb_ref, dtb_ref, alog_ref, dpair_ref, nw_ref, wout_ref,
                  l1g_ref, l1b_ref, o_ref,
                  h_s, cv_s, z_s, dt_s, u_s, c_s, xb_s, xc_s, y_s, mix_s, st_s):
    L = MIX_L
    d = x_ref.shape[-1]
    n_cb = d // LANES
    n_xb = xb_s.shape[0]
    n_sub = L // SSD_CHUNK

    @pl.when(pl.program_id(1) == 0)
    def _():
        u_s[:, 0:HIST_U, :] = jnp.zeros((n_cb, HIST_U, LANES), F32)
        xb_s[:, 0:HIST_X, :] = jnp.zeros((n_xb, HIST_X, LANES), F32)
        st_s[...] = jnp.zeros(st_s.shape, F32)

    sh1 = mod_ref[0, 0:1, :]
    sc1 = mod_ref[0, 1:2, :]
    g1 = mod_ref[0, 2:3, :]

    h_s[...] = (x_ref[0] * (1.0 + sc1) + sh1).astype(BF16)
    c0, c1, c2, c3 = 2 * d, 3 * d, 3 * d + n_xb * LANES, 3 * d + n_xb * LANES + LANES
    cv_s[...] = _dot(h_s[...], win_ref[:, 0:c0])
    z_s[...] = _dot(h_s[...], win_ref[:, c0:c1])
    xbc = _dot(h_s[...], win_ref[:, c1:c2])
    for cb in range(n_xb):
        xb_s[cb, HIST_X:HIST_X + L, :] = xbc[:, cb * LANES:(cb + 1) * LANES]
    dt_s[...] = _dot(h_s[...], win_ref[:, c2:c3])

    def glu(i, c):
        r0 = pl.multiple_of(i * ROWS, ROWS)
        val = cv_s[pl.ds(r0, ROWS), 0:d]
        gate = cv_s[pl.ds(r0, ROWS), d:2 * d]
        u = val * jax.nn.sigmoid(gate)
        for cb in range(n_cb):
            u_s[cb, pl.ds(pl.multiple_of(HIST_U + r0, SUBLANES), ROWS), :] = u[:, cb * LANES:(cb + 1) * LANES]
        return c
    lax.fori_loop(0, L // ROWS, glu, 0)

    def conv_u(idx, c):
        cb = idx // n_sub
        rb = idx - cb * n_sub
        r0 = pl.multiple_of(rb * SSD_CHUNK, SSD_CHUNK)
        wblk = ccw_ref[cb]
        acc = jnp.broadcast_to(ccb_ref[cb], (SSD_CHUNK, LANES))
        for k in range(CONF_KERNEL):
            off = HIST_U - (CONF_KERNEL - 1) + k
            acc = acc + wblk[k:k + 1, :] * u_s[cb, pl.ds(r0 + off, SSD_CHUNK), :]
        c_s[cb, pl.ds(r0, SSD_CHUNK), :] = acc
        return c
    lax.fori_loop(0, n_cb * n_sub, conv_u, 0)
    for cb in range(n_cb):
        u_s[cb, 0:HIST_U, :] = u_s[cb, L:L + HIST_U, :]

    def ln_u(i, c):
        r0 = pl.multiple_of(i * LN_ROWS, LN_ROWS)
        v = _cat([c_s[cb, pl.ds(r0, LN_ROWS), :] for cb in range(n_cb)])
        y = _ln(v, clg_ref[...], clb_ref[...])
        y = y * jax.nn.sigmoid(y)
        mix_s[pl.ds(r0, LN_ROWS), 0:d] = y.astype(BF16)
        return c
    lax.fori_loop(0, L // LN_ROWS, ln_u, 0)

    def conv_x(idx, c):
        cb = idx // n_sub
        rb = idx - cb * n_sub
        r0 = pl.multiple_of(rb * SSD_CHUNK, SSD_CHUNK)
        wblk = scw_ref[cb]
        acc = jnp.broadcast_to(scb_ref[cb], (SSD_CHUNK, LANES))
        for k in range(SSM_CONV):
            off = HIST_X - (SSM_CONV - 1) + k
            acc = acc + wblk[k:k + 1, :] * xb_s[cb, pl.ds(r0 + off, SSD_CHUNK), :]
        xc_s[cb, pl.ds(r0, SSD_CHUNK), :] = acc * jax.nn.sigmoid(acc)
        return c
    lax.fori_loop(0, n_xb * n_sub, conv_x, 0)
    for cb in range(n_xb):
        xb_s[cb, 0:HIST_X, :] = xb_s[cb, L:L + HIST_X, :]

    heads_per_group = SSM_HEADS // SSM_GROUPS
    row_i = lax.broadcasted_iota(I32, (SSD_CHUNK, SSD_CHUNK), 0)
    col_i = lax.broadcasted_iota(I32, (SSD_CHUNK, SSD_CHUNK), 1)
    causal = row_i >= col_i
    tri = jnp.where(causal, 1.0, 0.0).astype(BF16)
    low_half = col_i < SSM_HEAD_DIM
    a_neg = -jnp.exp(alog_ref[...])
    b_off = n_cb
    c_off = n_cb + SSM_GROUPS

    for j in range(n_sub):
        rs = slice(j * SSD_CHUNK, (j + 1) * SSD_CHUNK)
        dtr = dt_s[rs, :] + dtb_ref[...]
        dt = jnp.maximum(dtr, 0.0) + jnp.log1p(jnp.exp(-jnp.abs(dtr)))
        a3 = _split3(dt * a_neg)
        acs = _dot(tri, a3[0]) + _dot(tri, a3[1]) + _dot(tri, a3[2])
        acs_t = acs.T
        dt_t = dt.T
        last = acs[SSD_CHUNK - 1:SSD_CHUNK, :]
        e_all = jnp.exp(acs)
        w_all = dt * jnp.exp(last - acs)
        for g in range(SSM_GROUPS):
            bg = xc_s[b_off + g, rs, :].astype(BF16)
            cg = xc_s[c_off + g, rs, :].astype(BF16)
            cbm = lax.dot_general(cg, bg, _NT, preferred_element_type=F32)
            prev = st_s[g]
            yoff = _dot(cg, prev.astype(BF16))
            for half in range(heads_per_group // 2):
                pair = (g * heads_per_group) // 2 + half
                h0 = 2 * pair
                xs_pair = xc_s[pair, rs, :]
                ypair = None
                for q in range(2):
                    hh = h0 + q
                    seg = acs[:, hh:hh + 1] - acs_t[hh:hh + 1, :]
                    dec = jnp.exp(jnp.where(causal, seg, -jnp.inf))
                    m = (cbm * dec * dt_t[hh:hh + 1, :]).astype(BF16)
                    keep = low_half if q == 0 else jnp.logical_not(low_half)
                    rhs = jnp.where(keep, xs_pair, 0.0).astype(BF16)
                    part = _dot(m, rhs)
                    ypair = part if ypair is None else ypair + part
                e_pair = jnp.where(low_half, e_all[:, h0:h0 + 1], e_all[:, h0 + 1:h0 + 2])
                w_pair = jnp.where(low_half, w_all[:, h0:h0 + 1], w_all[:, h0 + 1:h0 + 2])
                cs = slice(half * LANES, (half + 1) * LANES)
                y_s[pair, rs, :] = ypair + yoff[:, cs] * e_pair + dpair_ref[pair] * xs_pair
                new = lax.dot_general(bg, (xs_pair * w_pair).astype(BF16), _TN,
                                      preferred_element_type=F32)
                st_s[g, :, cs] = e_pair[SSD_CHUNK - 1:SSD_CHUNK, :] * prev[:, cs] + new

    grp_w = d // SSM_GROUPS

    def gate_norm(i, c):
        r0 = pl.multiple_of(i * LN_ROWS, LN_ROWS)
        y = _cat([y_s[cb, pl.ds(r0, LN_ROWS), :] for cb in range(n_cb)])
        z = z_s[pl.ds(r0, LN_ROWS), :]
        gg = y * (z * jax.nn.sigmoid(z))
        outs = []
        for g in range(SSM_GROUPS):
            v = gg[:, g * grp_w:(g + 1) * grp_w]
            ms = jnp.mean(v * v, axis=-1, keepdims=True)
            outs.append(v * lax.rsqrt(ms + EPS))
        gn = _cat(outs) * nw_ref[...]
        mix_s[pl.ds(r0, LN_ROWS), d:2 * d] = gn.astype(BF16)
        return c
    lax.fori_loop(0, L // LN_ROWS, gate_norm, 0)

    cv_s[:, 0:d] = _dot(mix_s[...], wout_ref[...])

    def out_ln(i, c):
        r0 = pl.multiple_of(i * LN_ROWS, LN_ROWS)
        r = ALPHA * x_ref[0, pl.ds(r0, LN_ROWS), :] + g1 * cv_s[pl.ds(r0, LN_ROWS), 0:d]
        o_ref[0, pl.ds(r0, LN_ROWS), :] = _ln(r, l1g_ref[...], l1b_ref[...])
        return c
    lax.fori_loop(0, L // LN_ROWS, out_ln, 0)


def _const_spec(shape):
    nd = len(shape)
    return pl.BlockSpec(shape, lambda b, t: (0,) * nd)


def _mixer(x, mod3, w_in, conf_conv_w, conf_conv_b, conf_ln_g, conf_ln_b, ssm_conv_w, ssm_conv_b,
           ssm_dt_bias, ssm_A_log, ssm_D, ssm_norm_w, w_out, ln1_g, ln1_b):
    bsz, t, d = x.shape
    L = MIX_L
    xbc_dim = ssm_conv_w.shape[1]
    n_cb = d // LANES
    n_xb = xbc_dim // LANES
    in_cols = w_in.shape[1]
    win_p = jnp.pad(w_in, ((0, 0), (0, LANES - SSM_HEADS))).astype(BF16)
    ccw = jnp.pad(conf_conv_w, ((0, HIST_U - CONF_KERNEL), (0, 0))).reshape(HIST_U, n_cb, LANES).transpose(1, 0, 2)
    ccb = conf_conv_b.reshape(n_cb, 1, LANES)
    scw = jnp.pad(ssm_conv_w, ((0, SUBLANES - SSM_CONV), (0, 0))).reshape(SUBLANES, n_xb, LANES).transpose(1, 0, 2)
    scb = ssm_conv_b.reshape(n_xb, 1, LANES)
    dtb = jnp.pad(ssm_dt_bias, (0, LANES - SSM_HEADS)).reshape(1, LANES)
    alog = jnp.pad(ssm_A_log, (0, LANES - SSM_HEADS)).reshape(1, LANES)
    dpair = jnp.repeat(ssm_D, SSM_HEAD_DIM).reshape(n_cb, 1, LANES)
    row = lambda v: v.reshape(1, -1)

    in_specs = [
        pl.BlockSpec((1, L, d), lambda b, tt: (b, tt, 0)),
        pl.BlockSpec((1, 6, d), lambda b, tt: (b, 0, 0)),
        _const_spec((d, in_cols + LANES - SSM_HEADS)),
        _const_spec((n_cb, HIST_U, LANES)),
        _const_spec((n_cb, 1, LANES)),
        _const_spec((1, d)),
        _const_spec((1, d)),
        _const_spec((n_xb, SUBLANES, LANES)),
        _const_spec((n_xb, 1, LANES)),
        _const_spec((1, LANES)),
        _const_spec((1, LANES)),
        _const_spec((n_cb, 1, LANES)),
        _const_spec((1, d)),
        _const_spec((2 * d, d)),
        _const_spec((1, d)),
        _const_spec((1, d)),
    ]
    scratch = [
        pltpu.VMEM((L, d), BF16),
        pltpu.VMEM((L, 2 * d), F32),
        pltpu.VMEM((L, d), F32),
        pltpu.VMEM((L, LANES), F32),
        pltpu.VMEM((n_cb, HIST_U + L, LANES), F32),
        pltpu.VMEM((n_cb, L, LANES), F32),
        pltpu.VMEM((n_xb, HIST_X + L, LANES), F32),
        pltpu.VMEM((n_xb, L, LANES), F32),
        pltpu.VMEM((n_cb, L, LANES), F32),
        pltpu.VMEM((L, 2 * d), BF16),
        pltpu.VMEM((SSM_GROUPS, SSM_STATE, d // SSM_GROUPS), F32),
    ]
    return pl.pallas_call(
        _mixer_kernel,
        grid=(bsz, t // L),
        in_specs=in_specs,
        out_specs=pl.BlockSpec((1, L, d), lambda b, tt: (b, tt, 0)),
        out_shape=jax.ShapeDtypeStruct((bsz, t, d), F32),
        scratch_shapes=scratch,
        compiler_params=pltpu.CompilerParams(
            dimension_semantics=("arbitrary", "arbitrary"), vmem_limit_bytes=VMEM_LIMIT),
        name="mixer",
    )(x, mod3, win_p, ccw, ccb, row(conf_ln_g), row(conf_ln_b), scw, scb, dtb, alog, dpair,
      row(ssm_norm_w), w_out.astype(BF16), row(ln1_g), row(ln1_b))


def _route_kernel(x1_ref, mod_ref, wr_ref, br_ref, h2_ref, eidx_ref, rank_ref, gcol_ref, cnt_ref,
                  carry_s):
    tb = x1_ref.shape[0]

    @pl.when(pl.program_id(0) == 0)
    def _():
        carry_s[...] = jnp.zeros(carry_s.shape, F32)

    sh2 = mod_ref[0, 3:4, :]
    sc2 = mod_ref[0, 4:5, :]
    h2 = x1_ref[...] * (1.0 + sc2) + sh2
    for j in range(h2_ref.shape[0]):
        h2_ref[j] = h2[:, j * LANES:(j + 1) * LANES]

    w3 = _split3(wr_ref[...])
    h3 = _split3(h2)
    lg = None
    for i, j in ((0, 0), (0, 1), (1, 0), (0, 2), (2, 0), (1, 1)):
        part = lax.dot_general(w3[i], h3[j], _NT, preferred_element_type=F32)
        lg = part if lg is None else lg + part
    lg = lg + br_ref[...]

    epg = EXPERTS_PER_GROUP
    io8 = lax.broadcasted_iota(I32, (epg, tb), 0).astype(F32)

    def softmax0(v):
        e = jnp.exp(v - jnp.max(v, axis=0, keepdims=True))
        return e / jnp.sum(e, axis=0, keepdims=True)

    def top1(p):
        pm = jnp.max(p, axis=0, keepdims=True)
        idx = jnp.min(jnp.where(p == pm, io8, float(epg)), axis=0, keepdims=True)
        return pm, idx

    pg, gidx = top1(softmax0(lg[0:N_EXPERT_GROUPS, :]))
    sel = jnp.zeros((epg, tb), F32)
    for g in range(N_EXPERT_GROUPS):
        lo = N_EXPERT_GROUPS + g * epg
        sel = sel + jnp.where(gidx == float(g), lg[lo:lo + epg, :], 0.0)
    pe = softmax0(sel)
    p1, i1 = top1(pe)
    p2, i2 = top1(jnp.where(io8 == i1, -1.0, pe))
    den = p1 + p2
    gate0 = pg * (p1 / den)
    gate1 = pg * (p2 / den)
    e0 = (gidx * float(epg) + i1).astype(I32)
    e1 = (gidx * float(epg) + i2).astype(I32)

    io_e = lax.broadcasted_iota(I32, (N_EXPERTS, tb), 0)
    oh0 = jnp.where(io_e == e0, 1.0, 0.0)
    oh1 = jnp.where(io_e == e1, 1.0, 0.0)
    oh = oh0 + oh1
    rr = lax.broadcasted_iota(I32, (tb, tb), 0)
    cc = lax.broadcasted_iota(I32, (tb, tb), 1)
    before = jnp.where(rr < cc, 1.0, 0.0).astype(BF16)
    base = _dot(oh.astype(BF16), before) + carry_s[...]
    r0 = jnp.sum(oh0 * base, axis=0, keepdims=True)
    r1 = jnp.sum(oh1 * base, axis=0, keepdims=True)
    carry_s[...] = carry_s[...] + jnp.sum(oh, axis=1, keepdims=True)
    cnt_ref[...] = carry_s[...]

    eidx_ref[0:1, :] = e0
    eidx_ref[1:2, :] = e1
    rank_ref[0:1, :] = r0.astype(I32)
    rank_ref[1:2, :] = r1.astype(I32)
    io_l = lax.broadcasted_iota(I32, (LANES, tb), 0)
    gt = jnp.where(io_l == 0, gate0, jnp.where(io_l == 1, gate1, 0.0))
    gcol_ref[...] = gt.T


def _route(x1f, mod3, wr, br, tokens_per_seq):
    n, d = x1f.shape
    tb = ROUTE_TB
    per_seq = tokens_per_seq // tb
    return pl.pallas_call(
        _route_kernel,
        grid=(n // tb,),
        in_specs=[
            pl.BlockSpec((tb, d), lambda i: (i, 0)),
            pl.BlockSpec((1, 6, d), lambda i: (i // per_seq, 0, 0)),
            pl.BlockSpec((LANES, d), lambda i: (0, 0)),
            pl.BlockSpec((LANES, 1), lambda i: (0, 0)),
        ],
        out_specs=[
            pl.BlockSpec((d // LANES, tb, LANES), lambda i: (0, i, 0)),
            pl.BlockSpec((2, tb), lambda i: (0, i)),
            pl.BlockSpec((2, tb), lambda i: (0, i)),
            pl.BlockSpec((tb, LANES), lambda i: (i, 0)),
            pl.BlockSpec((N_EXPERTS, tb), lambda i: (0, 0)),
        ],
        out_shape=[
            jax.ShapeDtypeStruct((d // LANES, n, LANES), F32),
            jax.ShapeDtypeStruct((2, n), I32),
            jax.ShapeDtypeStruct((2, n), I32),
            jax.ShapeDtypeStruct((n, LANES), F32),
            jax.ShapeDtypeStruct((N_EXPERTS, tb), F32),
        ],
        scratch_shapes=[pltpu.VMEM((N_EXPERTS, tb), F32)],
        compiler_params=pltpu.CompilerParams(dimension_semantics=("arbitrary",)),
        name="router",
    )(x1f, mod3, wr, br)


def _on_slot(slot, fn):
    for s in range(2):
        @pl.when(slot == s)
        def _():
            fn(s)


class _RowGather:
    def __init__(self, src_hbm, idx_s, buf_s, isem, gsem, n_rows, priorities):
        self.src, self.idx_s, self.buf, self.isem, self.gsem, self.n = src_hbm, idx_s, buf_s, isem, gsem, n_rows
        self.priorities = priorities

    def _idx_copy(self, idx_vmem_ref, s):
        return pltpu.make_async_copy(idx_vmem_ref.at[0, 0], self.idx_s.at[pl.ds(s * self.n, self.n)],
                                     self.isem.at[s])

    def idx_start(self, idx_vmem_ref, s):
        self._idx_copy(idx_vmem_ref, s).start()

    def idx_wait(self, idx_vmem_ref, s):
        self._idx_copy(idx_vmem_ref, s).wait()

    def rows_start(self, s):
        def body(r8, c):
            for u in range(DMA_UNROLL):
                r = r8 * DMA_UNROLL + u
                row = self.idx_s[s * self.n + r]
                pltpu.make_async_copy(self.src.at[:, row], self.buf.at[s, :, r], self.gsem.at[s]).start(
                    priority=self.priorities[u % len(self.priorities)])
            return c
        lax.fori_loop(0, self.n // DMA_UNROLL, body, 0)

    def rows_wait(self, s):
        pltpu.make_async_copy(self.src.at[:, pl.ds(0, self.n)], self.buf.at[s], self.gsem.at[s]).wait()


def _ffn_kernel(bexp_ref, nexp_ref, nused_ref, idx_cur_ref, idx_nxt_ref, idx_nx2_ref, h2_hbm,
                w1_hbm, w3_hbm, w2_hbm,
                ys_ref, idx_s, run_s, xg_s, w1f, w3f, w2f, w1b, w3b, w2b, isem, gsem, wsem):
    i = pl.program_id(0)
    nb = pl.num_programs(0)
    nused = nused_ref[0]
    n_ch, bm = ys_ref.shape[0], ys_ref.shape[1]
    slot = lax.rem(i, 2)
    nslot = 1 - slot
    gather = _RowGather(h2_hbm, idx_s, xg_s, isem, gsem, bm, priorities=(1,))

    def weight_copies(e, s):
        return [pltpu.make_async_copy(src.at[e], dst.at[s], wsem.at[s])
                for src, dst in ((w1_hbm, w1f), (w3_hbm, w3f), (w2_hbm, w2f))]

    @pl.when(i == 0)
    def _():
        run_s[0] = 0
        gather.idx_start(idx_cur_ref, 0)
        gather.idx_wait(idx_cur_ref, 0)

        @pl.when(nused > 0)
        def _():
            for cp in weight_copies(bexp_ref[0], 0):
                cp.start()
            gather.rows_start(0)

        @pl.when(jnp.logical_and(1 < nb, 1 < nused))
        def _():
            gather.idx_start(idx_nxt_ref, 1)
            gather.idx_wait(idx_nxt_ref, 1)

    has_next = jnp.logical_and(i + 1 < nb, i + 1 < nused)
    has_next2 = jnp.logical_and(i + 2 < nb, i + 2 < nused)

    @pl.when(has_next)
    def _():
        _on_slot(nslot, gather.rows_start)

    @pl.when(has_next2)
    def _():
        _on_slot(slot, functools.partial(gather.idx_start, idx_nx2_ref))

    first_of_run = jnp.logical_or(i == 0, bexp_ref[i] != bexp_ref[jnp.maximum(i - 1, 0)])

    @pl.when(jnp.logical_and(first_of_run, i < nused))
    def _():
        e = bexp_ref[i]
        ne = nexp_ref[i]
        run = run_s[0]

        def load(ws):
            for cp in weight_copies(e, ws):
                cp.wait()

            @pl.when(ne != e)
            def _():
                for cp in weight_copies(ne, 1 - ws):
                    cp.start()
            w1b[...] = w1f[ws].astype(BF16)
            w3b[...] = w3f[ws].astype(BF16)
            w2b[...] = w2f[ws].astype(BF16)
        _on_slot(lax.rem(run, 2), load)
        run_s[0] = run + 1

    @pl.when(i < nused)
    def _():
        _on_slot(slot, gather.rows_wait)
        x = _cat([xg_s[slot, j] for j in range(n_ch)]).astype(BF16)
        a = _dot(x, w1b[...])
        b = _dot(x, w3b[...])
        hm = (a * jax.nn.sigmoid(a) * b).astype(BF16)
        y = _dot(hm, w2b[...])
        for j in range(n_ch):
            ys_ref[j] = y[:, j * LANES:(j + 1) * LANES]

    @pl.when(i >= nused)
    def _():
        ys_ref[...] = jnp.zeros(ys_ref.shape, F32)

    @pl.when(has_next2)
    def _():
        _on_slot(slot, functools.partial(gather.idx_wait, idx_nx2_ref))


def _ffn(block_exp, next_exp, n_used, slot_tok3, h2, w1, w3, w2):
    n_ch, n, _ = h2.shape
    d = n_ch * LANES
    n_blocks, _, bm = slot_tok3.shape
    de = w1.shape[-1]
    grid_spec = pltpu.PrefetchScalarGridSpec(
        num_scalar_prefetch=3,
        grid=(n_blocks,),
        in_specs=[
            pl.BlockSpec((1, 1, bm), lambda i, *_: (i, 0, 0)),
            pl.BlockSpec((1, 1, bm), lambda i, *_: (jnp.minimum(i + 1, n_blocks - 1), 0, 0)),
            pl.BlockSpec((1, 1, bm), lambda i, *_: (jnp.minimum(i + 2, n_blocks - 1), 0, 0)),
            pl.BlockSpec(memory_space=pl.ANY),
            pl.BlockSpec(memory_space=pl.ANY),
            pl.BlockSpec(memory_space=pl.ANY),
            pl.BlockSpec(memory_space=pl.ANY),
        ],
        out_specs=pl.BlockSpec((n_ch, bm, LANES), lambda i, *_: (0, i, 0)),
        scratch_shapes=[
            pltpu.SMEM((2 * bm,), I32),
            pltpu.SMEM((1,), I32),
            pltpu.VMEM((2, n_ch, bm, LANES), F32),
            pltpu.VMEM((2, d, de), F32),
            pltpu.VMEM((2, d, de), F32),
            pltpu.VMEM((2, de, d), F32),
            pltpu.VMEM((d, de), BF16),
            pltpu.VMEM((d, de), BF16),
            pltpu.VMEM((de, d), BF16),
            pltpu.SemaphoreType.DMA((2,)),
            pltpu.SemaphoreType.DMA((2,)),
            pltpu.SemaphoreType.DMA((2,)),
        ],
    )
    return pl.pallas_call(
        _ffn_kernel,
        grid_spec=grid_spec,
        out_shape=jax.ShapeDtypeStruct((n_ch, n_blocks * bm, LANES), F32),
        compiler_params=pltpu.CompilerParams(
            dimension_semantics=("arbitrary",), vmem_limit_bytes=VMEM_LIMIT),
        name="expert_ffn",
    )(block_exp, next_exp, n_used, slot_tok3, slot_tok3, slot_tok3, h2, w1, w3, w2)


def _combine_kernel(idx_cur_ref, idx_nxt_ref, x1_ref, mod_ref, gcol_ref, ys_hbm, l2g_ref, l2b_ref,
                    o_ref, idx_s, yg_s, isem, gsem):
    i = pl.program_id(0)
    nb = pl.num_programs(0)
    tc = x1_ref.shape[0]
    n_ch = yg_s.shape[1]
    slot = lax.rem(i, 2)
    nslot = 1 - slot
    gather = _RowGather(ys_hbm, idx_s, yg_s, isem, gsem, 2 * tc, priorities=(0, 1))

    @pl.when(i == 0)
    def _():
        gather.idx_start(idx_cur_ref, 0)
        gather.idx_wait(idx_cur_ref, 0)
        gather.rows_start(0)

    @pl.when(i + 1 < nb)
    def _():
        _on_slot(nslot, functools.partial(gather.idx_start, idx_nxt_ref))

    _on_slot(slot, gather.rows_wait)

    @pl.when(i + 1 < nb)
    def _():
        def nxt(s):
            gather.idx_wait(idx_nxt_ref, s)
            gather.rows_start(s)
        _on_slot(nslot, nxt)

    g2 = mod_ref[0, 5:6, :]

    def rows_of(first, r0):
        start = pl.multiple_of(first + r0, LN_ROWS)
        return _cat([yg_s[slot, j, pl.ds(start, LN_ROWS), :] for j in range(n_ch)])

    def body(j, c):
        r0 = pl.multiple_of(j * LN_ROWS, LN_ROWS)
        gc = gcol_ref[pl.ds(r0, LN_ROWS), :]
        y0 = rows_of(0, r0)
        y1 = rows_of(tc, r0)
        ffn = gc[:, 0:1] * y0 + gc[:, 1:2] * y1
        r = ALPHA * x1_ref[pl.ds(r0, LN_ROWS), :] + g2 * ffn
        o_ref[pl.ds(r0, LN_ROWS), :] = _ln(r, l2g_ref[...], l2b_ref[...])
        return c
    lax.fori_loop(0, tc // LN_ROWS, body, 0)


def _combine(dest3, x1f, mod3, gcol, ys, ln2_g, ln2_b, tokens_per_seq):
    n, d = x1f.shape
    tc = COMB_TC
    nb = n // tc
    per_seq = tokens_per_seq // tc
    return pl.pallas_call(
        _combine_kernel,
        grid=(nb,),
        in_specs=[
            pl.BlockSpec((1, 1, 2 * tc), lambda i: (i, 0, 0)),
            pl.BlockSpec((1, 1, 2 * tc), lambda i: (jnp.minimum(i + 1, nb - 1), 0, 0)),
            pl.BlockSpec((tc, d), lambda i: (i, 0)),
            pl.BlockSpec((1, 6, d), lambda i: (i // per_seq, 0, 0)),
            pl.BlockSpec((tc, LANES), lambda i: (i, 0)),
            pl.BlockSpec(memory_space=pl.ANY),
            pl.BlockSpec((1, d), lambda i: (0, 0)),
            pl.BlockSpec((1, d), lambda i: (0, 0)),
        ],
        out_specs=pl.BlockSpec((tc, d), lambda i: (i, 0)),
        out_shape=jax.ShapeDtypeStruct((n, d), F32),
        scratch_shapes=[
            pltpu.SMEM((4 * tc,), I32),
            pltpu.VMEM((2, d // LANES, 2 * tc, LANES), F32),
            pltpu.SemaphoreType.DMA((2,)),
            pltpu.SemaphoreType.DMA((2,)),
        ],
        compiler_params=pltpu.CompilerParams(
            dimension_semantics=("arbitrary",), vmem_limit_bytes=VMEM_LIMIT),
        name="combine",
    )(dest3, dest3, x1f, mod3, gcol, ys, ln2_g.reshape(1, d), ln2_b.reshape(1, d))


def _layer(x, mod3, l, w_in, conf_conv_w, conf_conv_b, conf_ln_g, conf_ln_b, ssm_conv_w, ssm_conv_b,
           ssm_dt_bias, ssm_A_log, ssm_D, ssm_norm_w, w_out, ln1_g, ln1_b, router_group_w,
           router_group_b, router_expert_w, router_expert_b, expert_w1, expert_w3, expert_w2,
           ln2_g, ln2_b):
    bsz, t, d = x.shape
    n = bsz * t
    x1 = _mixer(x, mod3, w_in[l], conf_conv_w[l], conf_conv_b[l], conf_ln_g[l], conf_ln_b[l],
                ssm_conv_w[l], ssm_conv_b[l], ssm_dt_bias[l], ssm_A_log[l], ssm_D[l], ssm_norm_w[l],
                w_out[l], ln1_g[l], ln1_b[l])
    x1f = x1.reshape(n, d)

    n_router = N_EXPERT_GROUPS + N_EXPERTS
    wr = jnp.concatenate([router_group_w[l], router_expert_w[l]], axis=1).T
    wr = jnp.pad(wr, ((0, LANES - n_router), (0, 0)))
    br = jnp.pad(jnp.concatenate([router_group_b[l], router_expert_b[l]]), (0, LANES - n_router))
    h2, eidx, rank, gcol, cnt = _route(x1f, mod3, wr, br.reshape(LANES, 1), t)

    bm = FFN_BM
    e_ids = jnp.arange(N_EXPERTS, dtype=I32)
    counts = cnt[:, 0].astype(I32)
    padded = ((counts + bm - 1) // bm) * bm
    pad_ends = jnp.cumsum(padded)
    pad_starts = pad_ends - padded
    n_slots = ((2 * n + bm - 1) // bm) * bm + N_EXPERTS * bm
    n_blocks = n_slots // bm
    dest = rank + jnp.sum(jnp.where(eidx[None] == e_ids[:, None, None], pad_starts[:, None, None], 0), axis=0)
    tok = jnp.broadcast_to(jnp.arange(n, dtype=I32), (2, n))
    slot_tok = jnp.zeros((n_slots,), I32).at[dest.reshape(-1)].set(
        tok.reshape(-1), unique_indices=True, indices_are_sorted=False, mode="promise_in_bounds")
    block_start = jnp.arange(n_blocks, dtype=I32) * bm
    block_exp = jnp.minimum(jnp.sum((pad_ends[None, :] <= block_start[:, None]).astype(I32), axis=1),
                            N_EXPERTS - 1)
    n_used = (pad_ends[-1:] // bm).astype(I32)
    later = jnp.where((padded > 0)[None, :] & (e_ids[None, :] > e_ids[:, None]), e_ids[None, :], N_EXPERTS)
    nxt = jnp.min(later, axis=1)
    next_of_expert = jnp.where(nxt < N_EXPERTS, nxt, e_ids)
    next_exp = jnp.sum(jnp.where(block_exp[:, None] == e_ids[None, :], next_of_expert[None, :], 0), axis=1)

    ys = _ffn(block_exp, next_exp.astype(I32), n_used, slot_tok.reshape(n_blocks, 1, bm), h2,
              expert_w1[l], expert_w3[l], expert_w2[l])

    tc = COMB_TC
    dest3 = dest.reshape(2, n // tc, tc).transpose(1, 0, 2).reshape(n // tc, 1, 2 * tc)
    out = _combine(dest3, x1f, mod3, gcol, ys, ln2_g[l], ln2_b[l], t)
    return out.reshape(bsz, t, d)


def kernel(x, c, w_ada, b_ada, w_in, conf_conv_w, conf_conv_b, conf_ln_g, conf_ln_b, ssm_conv_w,
           ssm_conv_b, ssm_dt_bias, ssm_A_log, ssm_D, ssm_norm_w, w_out, ln1_g, ln1_b,
           router_group_w, router_group_b, router_expert_w, router_expert_b, expert_w1, expert_w3,
           expert_w2, ln2_g, ln2_b):
    bsz, t, d = x.shape
    for l in range(w_ada.shape[0]):
        mod3 = _ada(c, w_ada[l], b_ada[l]).reshape(bsz, 6, d)
        x = _layer(x, mod3, l, w_in, conf_conv_w, conf_conv_b, conf_ln_g, conf_ln_b, ssm_conv_w,
                   ssm_conv_b, ssm_dt_bias, ssm_A_log, ssm_D, ssm_norm_w, w_out, ln1_g, ln1_b,
                   router_group_w, router_group_b, router_expert_w, router_expert_b, expert_w1,
                   expert_w3, expert_w2, ln2_g, ln2_b)
    return x
```

```python
import functools

import jax
import jax.numpy as jnp
from jax import lax
from jax.experimental import pallas as pl
from jax.experimental.pallas import tpu as pltpu

F32 = jnp.float32
BF16 = jnp.bfloat16
I32 = jnp.int32

EPS = 1e-5
LANES = 128
SUBLANES = 8
VMEM_LIMIT = 56 * 1024 * 1024

CONF_KERNEL = 31
SSM_HEADS = 16
SSM_HEAD_DIM = 64
SSM_GROUPS = 4
SSM_STATE = 128
SSM_CONV = 4
SSD_CHUNK = 128
N_EXPERT_GROUPS = 8
EXPERTS_PER_GROUP = 8
N_EXPERTS = N_EXPERT_GROUPS * EXPERTS_PER_GROUP
DEPTH = 1
ALPHA = (2.0 * DEPTH) ** 0.25

MIX_L = 256
ROWS = 32
LN_ROWS = 128
HIST_U = 32
HIST_X = 8
ROUTE_TB = 256
FFN_BM = 512
COMB_TC = 512
DMA_UNROLL = 8


def _ln(v, g, b):
    mu = jnp.mean(v, axis=-1, keepdims=True)
    d = v - mu
    var = jnp.mean(d * d, axis=-1, keepdims=True)
    return d * lax.rsqrt(var + EPS) * g + b


def _split3(a):
    hi = a.astype(BF16)
    r1 = a - hi.astype(F32)
    mid = r1.astype(BF16)
    lo = (r1 - mid.astype(F32)).astype(BF16)
    return hi, mid, lo


def _dot(a, b):
    return jnp.dot(a, b, preferred_element_type=F32)


def _cat(parts):
    return jnp.concatenate(parts, axis=1)


_NT = (((1,), (1,)), ((), ()))
_TN = (((0,), (0,)), ((), ()))


def _ada_kernel(c_ref, w_ref, b_ref, o_ref):
    c = c_ref[...]
    cond = c * jax.nn.sigmoid(c)
    c3 = _split3(cond)
    w3 = _split3(w_ref[...])
    acc = _dot(c3[0], w3[0])
    for i, j in ((0, 1), (1, 0), (0, 2), (2, 0), (1, 1)):
        acc = acc + _dot(c3[i], w3[j])
    o_ref[...] = acc + b_ref[...]


def _ada(c, w_ada, b_ada):
    bsz, d = c.shape
    n = w_ada.shape[1]
    return pl.pallas_call(
        _ada_kernel,
        grid=(n // d,),
        in_specs=[
            pl.BlockSpec((bsz, d), lambda j: (0, 0)),
            pl.BlockSpec((d, d), lambda j: (0, j)),
            pl.BlockSpec((1, d), lambda j: (0, j)),
        ],
        out_specs=pl.BlockSpec((bsz, d), lambda j: (0, j)),
        out_shape=jax.ShapeDtypeStruct((bsz, n), F32),
        compiler_params=pltpu.CompilerParams(dimension_semantics=("arbitrary",)),
        name="adaln",
    )(c, w_ada, b_ada.reshape(1, n))


def _mixer_kernel(x_ref, mod_ref, win_ref, ccw_ref, ccb_ref, clg_ref, clb_ref,
                  scw_ref, scb_ref, dtb_ref, alog_ref, dpair_ref, nw_ref, wout_ref,
                  l1g_ref, l1b_ref, o_ref,
                  h_s, cv_s, z_s, dt_s, u_s, c_s, xb_s, xc_s, y_s, mix_s, st_s):
    L = MIX_L
    d = x_ref.shape[-1]
    n_cb = d // LANES
    n_xb = xb_s.shape[0]
    n_sub = L // SSD_CHUNK

    @pl.when(pl.program_id(1) == 0)
    def _():
        u_s[:, 0:HIST_U, :] = jnp.zeros((n_cb, HIST_U, LANES), F32)
        xb_s[:, 0:HIST_X, :] = jnp.zeros((n_xb, HIST_X, LANES), F32)
        st_s[...] = jnp.zeros(st_s.shape, F32)

    sh1 = mod_ref[0, 0:1, :]
    sc1 = mod_ref[0, 1:2, :]
    g1 = mod_ref[0, 2:3, :]

    h_s[...] = (x_ref[0] * (1.0 + sc1) + sh1).astype(BF16)
    c0, c1, c2, c3 = 2 * d, 3 * d, 3 * d + n_xb * LANES, 3 * d + n_xb * LANES + LANES
    cv_s[...] = _dot(h_s[...], win_ref[:, 0:c0])
    z_s[...] = _dot(h_s[...], win_ref[:, c0:c1])
    xbc = _dot(h_s[...], win_ref[:, c1:c2])
    for cb in range(n_xb):
        xb_s[cb, HIST_X:HIST_X + L, :] = xbc[:, cb * LANES:(cb + 1) * LANES]
    dt_s[...] = _dot(h_s[...], win_ref[:, c2:c3])

    def glu(i, c):
        r0 = pl.multiple_of(i * ROWS, ROWS)
        val = cv_s[pl.ds(r0, ROWS), 0:d]
        gate = cv_s[pl.ds(r0, ROWS), d:2 * d]
        u = val * jax.nn.sigmoid(gate)
        for cb in range(n_cb):
            u_s[cb, pl.ds(pl.multiple_of(HIST_U + r0, SUBLANES), ROWS), :] = u[:, cb * LANES:(cb + 1) * LANES]
        return c
    lax.fori_loop(0, L // ROWS, glu, 0)

    def conv_u(idx, c):
        cb = idx // n_sub
        rb = idx - cb * n_sub
        r0 = pl.multiple_of(rb * SSD_CHUNK, SSD_CHUNK)
        wblk = ccw_ref[cb]
        acc = jnp.broadcast_to(ccb_ref[cb], (SSD_CHUNK, LANES))
        for k in range(CONF_KERNEL):
            off = HIST_U - (CONF_KERNEL - 1) + k
            acc = acc + wblk[k:k + 1, :] * u_s[cb, pl.ds(r0 + off, SSD_CHUNK), :]
        c_s[cb, pl.ds(r0, SSD_CHUNK), :] = acc
        return c
    lax.fori_loop(0, n_cb * n_sub, conv_u, 0)
    for cb in range(n_cb):
        u_s[cb, 0:HIST_U, :] = u_s[cb, L:L + HIST_U, :]

    def ln_u(i, c):
        r0 = pl.multiple_of(i * LN_ROWS, LN_ROWS)
        v = _cat([c_s[cb, pl.ds(r0, LN_ROWS), :] for cb in range(n_cb)])
        y = _ln(v, clg_ref[...], clb_ref[...])
        y = y * jax.nn.sigmoid(y)
        mix_s[pl.ds(r0, LN_ROWS), 0:d] = y.astype(BF16)
        return c
    lax.fori_loop(0, L // LN_ROWS, ln_u, 0)

    def conv_x(idx, c):
        cb = idx // n_sub
        rb = idx - cb * n_sub
        r0 = pl.multiple_of(rb * SSD_CHUNK, SSD_CHUNK)
        wblk = scw_ref[cb]
        acc = jnp.broadcast_to(scb_ref[cb], (SSD_CHUNK, LANES))
        for k in range(SSM_CONV):
            off = HIST_X - (SSM_CONV - 1) + k
            acc = acc + wblk[k:k + 1, :] * xb_s[cb, pl.ds(r0 + off, SSD_CHUNK), :]
        xc_s[cb, pl.ds(r0, SSD_CHUNK), :] = acc * jax.nn.sigmoid(acc)
        return c
    lax.fori_loop(0, n_xb * n_sub, conv_x, 0)
    for cb in range(n_xb):
        xb_s[cb, 0:HIST_X, :] = xb_s[cb, L:L + HIST_X, :]

    heads_per_group = SSM_HEADS // SSM_GROUPS
    row_i = lax.broadcasted_iota(I32, (SSD_CHUNK, SSD_CHUNK), 0)
    col_i = lax.broadcasted_iota(I32, (SSD_CHUNK, SSD_CHUNK), 1)
    causal = row_i >= col_i
    tri = jnp.where(causal, 1.0, 0.0).astype(BF16)
    low_half = col_i < SSM_HEAD_DIM
    a_neg = -jnp.exp(alog_ref[...])
    b_off = n_cb
    c_off = n_cb + SSM_GROUPS

    for j in range(n_sub):
        rs = slice(j * SSD_CHUNK, (j + 1) * SSD_CHUNK)
        dtr = dt_s[rs, :] + dtb_ref[...]
        dt = jnp.maximum(dtr, 0.0) + jnp.log1p(jnp.exp(-jnp.abs(dtr)))
        a3 = _split3(dt * a_neg)
        acs = _dot(tri, a3[0]) + _dot(tri, a3[1]) + _dot(tri, a3[2])
        acs_t = acs.T
        dt_t = dt.T
        last = acs[SSD_CHUNK - 1:SSD_CHUNK, :]
        e_all = jnp.exp(acs)
        w_all = dt * jnp.exp(last - acs)
        for g in range(SSM_GROUPS):
            bg = xc_s[b_off + g, rs, :].astype(BF16)
            cg = xc_s[c_off + g, rs, :].astype(BF16)
            cbm = lax.dot_general(cg, bg, _NT, preferred_element_type=F32)
            prev = st_s[g]
            yoff = _dot(cg, prev.astype(BF16))
            for half in range(heads_per_group // 2):
                pair = (g * heads_per_group) // 2 + half
                h0 = 2 * pair
                xs_pair = xc_s[pair, rs, :]
                ypair = None
                for q in range(2):
                    hh = h0 + q
                    seg = acs[:, hh:hh + 1] - acs_t[hh:hh + 1, :]
                    dec = jnp.exp(jnp.where(causal, seg, -jnp.inf))
                    m = (cbm * dec * dt_t[hh:hh + 1, :]).astype(BF16)
                    keep = low_half if q == 0 else jnp.logical_not(low_half)
                    rhs = jnp.where(keep, xs_pair, 0.0).astype(BF16)
                    part = _dot(m, rhs)
                    ypair = part if ypair is None else ypair + part
                e_pair = jnp.where(low_half, e_all[:, h0:h0 + 1], e_all[:, h0 + 1:h0 + 2])
                w_pair = jnp.where(low_half, w_all[:, h0:h0 + 1], w_all[:, h0 + 1:h0 + 2])
                cs = slice(half * LANES, (half + 1) * LANES)
                y_s[pair, rs, :] = ypair + yoff[:, cs] * e_pair + dpair_ref[pair] * xs_pair
                new = lax.dot_general(bg, (xs_pair * w_pair).astype(BF16), _TN,
                                      preferred_element_type=F32)
                st_s[g, :, cs] = e_pair[SSD_CHUNK - 1:SSD_CHUNK, :] * prev[:, cs] + new

    grp_w = d // SSM_GROUPS

    def gate_norm(i, c):
        r0 = pl.multiple_of(i * LN_ROWS, LN_ROWS)
        y = _cat([y_s[cb, pl.ds(r0, LN_ROWS), :] for cb in range(n_cb)])
        z = z_s[pl.ds(r0, LN_ROWS), :]
        gg = y * (z * jax.nn.sigmoid(z))
        outs = []
        for g in range(SSM_GROUPS):
            v = gg[:, g * grp_w:(g + 1) * grp_w]
            ms = jnp.mean(v * v, axis=-1, keepdims=True)
            outs.append(v * lax.rsqrt(ms + EPS))
        gn = _cat(outs) * nw_ref[...]
        mix_s[pl.ds(r0, LN_ROWS), d:2 * d] = gn.astype(BF16)
        return c
    lax.fori_loop(0, L // LN_ROWS, gate_norm, 0)

    cv_s[:, 0:d] = _dot(mix_s[...], wout_ref[...])

    def out_ln(i, c):
        r0 = pl.multiple_of(i * LN_ROWS, LN_ROWS)
        r = ALPHA * x_ref[0, pl.ds(r0, LN_ROWS), :] + g1 * cv_s[pl.ds(r0, LN_ROWS), 0:d]
        o_ref[0, pl.ds(r0, LN_ROWS), :] = _ln(r, l1g_ref[...], l1b_ref[...])
        return c
    lax.fori_loop(0, L // LN_ROWS, out_ln, 0)


def _const_spec(shape):
    nd = len(shape)
    return pl.BlockSpec(shape, lambda b, t: (0,) * nd)


def _mixer(x, mod3, w_in, conf_conv_w, conf_conv_b, conf_ln_g, conf_ln_b, ssm_conv_w, ssm_conv_b,
           ssm_dt_bias, ssm_A_log, ssm_D, ssm_norm_w, w_out, ln1_g, ln1_b):
    bsz, t, d = x.shape
    L = MIX_L
    xbc_dim = ssm_conv_w.shape[1]
    n_cb = d // LANES
    n_xb = xbc_dim // LANES
    in_cols = w_in.shape[1]
    win_p = jnp.pad(w_in, ((0, 0), (0, LANES - SSM_HEADS))).astype(BF16)
    ccw = jnp.pad(conf_conv_w, ((0, HIST_U - CONF_KERNEL), (0, 0))).reshape(HIST_U, n_cb, LANES).transpose(1, 0, 2)
    ccb = conf_conv_b.reshape(n_cb, 1, LANES)
    scw = jnp.pad(ssm_conv_w, ((0, SUBLANES - SSM_CONV), (0, 0))).reshape(SUBLANES, n_xb, LANES).transpose(1, 0, 2)
    scb = ssm_conv_b.reshape(n_xb, 1, LANES)
    dtb = jnp.pad(ssm_dt_bias, (0, LANES - SSM_HEADS)).reshape(1, LANES)
    alog = jnp.pad(ssm_A_log, (0, LANES - SSM_HEADS)).reshape(1, LANES)
    dpair = jnp.repeat(ssm_D, SSM_HEAD_DIM).reshape(n_cb, 1, LANES)
    row = lambda v: v.reshape(1, -1)

    in_specs = [
        pl.BlockSpec((1, L, d), lambda b, tt: (b, tt, 0)),
        pl.BlockSpec((1, 6, d), lambda b, tt: (b, 0, 0)),
        _const_spec((d, in_cols + LANES - SSM_HEADS)),
        _const_spec((n_cb, HIST_U, LANES)),
        _const_spec((n_cb, 1, LANES)),
        _const_spec((1, d)),
        _const_spec((1, d)),
        _const_spec((n_xb, SUBLANES, LANES)),
        _const_spec((n_xb, 1, LANES)),
        _const_spec((1, LANES)),
        _const_spec((1, LANES)),
        _const_spec((n_cb, 1, LANES)),
        _const_spec((1, d)),
        _const_spec((2 * d, d)),
        _const_spec((1, d)),
        _const_spec((1, d)),
    ]
    scratch = [
        pltpu.VMEM((L, d), BF16),
        pltpu.VMEM((L, 2 * d), F32),
        pltpu.VMEM((L, d), F32),
        pltpu.VMEM((L, LANES), F32),
        pltpu.VMEM((n_cb, HIST_U + L, LANES), F32),
        pltpu.VMEM((n_cb, L, LANES), F32),
        pltpu.VMEM((n_xb, HIST_X + L, LANES), F32),
        pltpu.VMEM((n_xb, L, LANES), F32),
        pltpu.VMEM((n_cb, L, LANES), F32),
        pltpu.VMEM((L, 2 * d), BF16),
        pltpu.VMEM((SSM_GROUPS, SSM_STATE, d // SSM_GROUPS), F32),
    ]
    return pl.pallas_call(
        _mixer_kernel,
        grid=(bsz, t // L),
        in_specs=in_specs,
        out_specs=pl.BlockSpec((1, L, d), lambda b, tt: (b, tt, 0)),
        out_shape=jax.ShapeDtypeStruct((bsz, t, d), F32),
        scratch_shapes=scratch,
        compiler_params=pltpu.CompilerParams(
            dimension_semantics=("arbitrary", "arbitrary"), vmem_limit_bytes=VMEM_LIMIT),
        name="mixer",
    )(x, mod3, win_p, ccw, ccb, row(conf_ln_g), row(conf_ln_b), scw, scb, dtb, alog, dpair,
      row(ssm_norm_w), w_out.astype(BF16), row(ln1_g), row(ln1_b))


def _route_kernel(x1_ref, mod_ref, wr_ref, br_ref, h2_ref, eidx_ref, rank_ref, gcol_ref, cnt_ref,
                  carry_s):
    tb = x1_ref.shape[0]

    @pl.when(pl.program_id(0) == 0)
    def _():
        carry_s[...] = jnp.zeros(carry_s.shape, F32)

    sh2 = mod_ref[0, 3:4, :]
    sc2 = mod_ref[0, 4:5, :]
    h2 = x1_ref[...] * (1.0 + sc2) + sh2
    for j in range(h2_ref.shape[0]):
        h2_ref[j] = h2[:, j * LANES:(j + 1) * LANES]

    w3 = _split3(wr_ref[...])
    h3 = _split3(h2)
    lg = None
    for i, j in ((0, 0), (0, 1), (1, 0), (0, 2), (2, 0), (1, 1)):
        part = lax.dot_general(w3[i], h3[j], _NT, preferred_element_type=F32)
        lg = part if lg is None else lg + part
    lg = lg + br_ref[...]

    epg = EXPERTS_PER_GROUP
    io8 = lax.broadcasted_iota(I32, (epg, tb), 0).astype(F32)

    def softmax0(v):
        e = jnp.exp(v - jnp.max(v, axis=0, keepdims=True))
        return e / jnp.sum(e, axis=0, keepdims=True)

    def top1(p):
        pm = jnp.max(p, axis=0, keepdims=True)
        idx = jnp.min(jnp.where(p == pm, io8, float(epg)), axis=0, keepdims=True)
        return pm, idx

    pg, gidx = top1(softmax0(lg[0:N_EXPERT_GROUPS, :]))
    sel = jnp.zeros((epg, tb), F32)
    for g in range(N_EXPERT_GROUPS):
        lo = N_EXPERT_GROUPS + g * epg
        sel = sel + jnp.where(gidx == float(g), lg[lo:lo + epg, :], 0.0)
    pe = softmax0(sel)
    p1, i1 = top1(pe)
    p2, i2 = top1(jnp.where(io8 == i1, -1.0, pe))
    den = p1 + p2
    gate0 = pg * (p1 / den)
    gate1 = pg * (p2 / den)
    e0 = (gidx * float(epg) + i1).astype(I32)
    e1 = (gidx * float(epg) + i2).astype(I32)

    io_e = lax.broadcasted_iota(I32, (N_EXPERTS, tb), 0)
    oh0 = jnp.where(io_e == e0, 1.0, 0.0)
    oh1 = jnp.where(io_e == e1, 1.0, 0.0)
    oh = oh0 + oh1
    rr = lax.broadcasted_iota(I32, (tb, tb), 0)
    cc = lax.broadcasted_iota(I32, (tb, tb), 1)
    before = jnp.where(rr < cc, 1.0, 0.0).astype(BF16)
    base = _dot(oh.astype(BF16), before) + carry_s[...]
    r0 = jnp.sum(oh0 * base, axis=0, keepdims=True)
    r1 = jnp.sum(oh1 * base, axis=0, keepdims=True)
    carry_s[...] = carry_s[...] + jnp.sum(oh, axis=1, keepdims=True)
    cnt_ref[...] = carry_s[...]

    eidx_ref[0:1, :] = e0
    eidx_ref[1:2, :] = e1
    rank_ref[0:1, :] = r0.astype(I32)
    rank_ref[1:2, :] = r1.astype(I32)
    io_l = lax.broadcasted_iota(I32, (LANES, tb), 0)
    gt = jnp.where(io_l == 0, gate0, jnp.where(io_l == 1, gate1, 0.0))
    gcol_ref[...] = gt.T


def _route(x1f, mod3, wr, br, tokens_per_seq):
    n, d = x1f.shape
    tb = ROUTE_TB
    per_seq = tokens_per_seq // tb
    return pl.pallas_call(
        _route_kernel,
        grid=(n // tb,),
        in_specs=[
            pl.BlockSpec((tb, d), lambda i: (i, 0)),
            pl.BlockSpec((1, 6, d), lambda i: (i // per_seq, 0, 0)),
            pl.BlockSpec((LANES, d), lambda i: (0, 0)),
            pl.BlockSpec((LANES, 1), lambda i: (0, 0)),
        ],
        out_specs=[
            pl.BlockSpec((d // LANES, tb, LANES), lambda i: (0, i, 0)),
            pl.BlockSpec((2, tb), lambda i: (0, i)),
            pl.BlockSpec((2, tb), lambda i: (0, i)),
            pl.BlockSpec((tb, LANES), lambda i: (i, 0)),
            pl.BlockSpec((N_EXPERTS, tb), lambda i: (0, 0)),
        ],
        out_shape=[
            jax.ShapeDtypeStruct((d // LANES, n, LANES), F32),
            jax.ShapeDtypeStruct((2, n), I32),
            jax.ShapeDtypeStruct((2, n), I32),
            jax.ShapeDtypeStruct((n, LANES), F32),
            jax.ShapeDtypeStruct((N_EXPERTS, tb), F32),
        ],
        scratch_shapes=[pltpu.VMEM((N_EXPERTS, tb), F32)],
        compiler_params=pltpu.CompilerParams(dimension_semantics=("arbitrary",)),
        name="router",
    )(x1f, mod3, wr, br)


def _on_slot(slot, fn):
    for s in range(2):
        @pl.when(slot == s)
        def _():
            fn(s)


class _RowGather:
    def __init__(self, src_hbm, idx_s, buf_s, isem, gsem, n_rows, priorities):
        self.src, self.idx_s, self.buf, self.isem, self.gsem, self.n = src_hbm, idx_s, buf_s, isem, gsem, n_rows
        self.priorities = priorities

    def _idx_copy(self, idx_vmem_ref, s):
        return pltpu.make_async_copy(idx_vmem_ref.at[0, 0], self.idx_s.at[pl.ds(s * self.n, self.n)],
                                     self.isem.at[s])

    def idx_start(self, idx_vmem_ref, s):
        self._idx_copy(idx_vmem_ref, s).start()

    def idx_wait(self, idx_vmem_ref, s):
        self._idx_copy(idx_vmem_ref, s).wait()

    def rows_start(self, s):
        def body(r8, c):
            for u in range(DMA_UNROLL):
                r = r8 * DMA_UNROLL + u
                row = self.idx_s[s * self.n + r]
                pltpu.make_async_copy(self.src.at[:, row], self.buf.at[s, :, r], self.gsem.at[s]).start(
                    priority=self.priorities[u % len(self.priorities)])
            return c
        lax.fori_loop(0, self.n // DMA_UNROLL, body, 0)

    def rows_wait(self, s):
        pltpu.make_async_copy(self.src.at[:, pl.ds(0, self.n)], self.buf.at[s], self.gsem.at[s]).wait()


def _ffn_kernel(bexp_ref, nexp_ref, nused_ref, idx_cur_ref, idx_nxt_ref, idx_nx2_ref, h2_hbm,
                w1_hbm, w3_hbm, w2_hbm,
                ys_ref, idx_s, run_s, xg_s, w1f, w3f, w2f, w1b, w3b, w2b, isem, gsem, wsem):
    i = pl.program_id(0)
    nb = pl.num_programs(0)
    nused = nused_ref[0]
    n_ch, bm = ys_ref.shape[0], ys_ref.shape[1]
    slot = lax.rem(i, 2)
    nslot = 1 - slot
    gather = _RowGather(h2_hbm, idx_s, xg_s, isem, gsem, bm, priorities=(1,))

    def weight_copies(e, s):
        return [pltpu.make_async_copy(src.at[e], dst.at[s], wsem.at[s])
                for src, dst in ((w1_hbm, w1f), (w3_hbm, w3f), (w2_hbm, w2f))]

    @pl.when(i == 0)
    def _():
        run_s[0] = 0
        gather.idx_start(idx_cur_ref, 0)
        gather.idx_wait(idx_cur_ref, 0)

        @pl.when(nused > 0)
        def _():
            for cp in weight_copies(bexp_ref[0], 0):
                cp.start()
            gather.rows_start(0)

        @pl.when(jnp.logical_and(1 < nb, 1 < nused))
        def _():
            gather.idx_start(idx_nxt_ref, 1)
            gather.idx_wait(idx_nxt_ref, 1)

    has_next = jnp.logical_and(i + 1 < nb, i + 1 < nused)
    has_next2 = jnp.logical_and(i + 2 < nb, i + 2 < nused)

    @pl.when(has_next)
    def _():
        _on_slot(nslot, gather.rows_start)

    @pl.when(has_next2)
    def _():
        _on_slot(slot, functools.partial(gather.idx_start, idx_nx2_ref))

    first_of_run = jnp.logical_or(i == 0, bexp_ref[i] != bexp_ref[jnp.maximum(i - 1, 0)])

    @pl.when(jnp.logical_and(first_of_run, i < nused))
    def _():
        e = bexp_ref[i]
        ne = nexp_ref[i]
        run = run_s[0]

        def load(ws):
            for cp in weight_copies(e, ws):
                cp.wait()

            @pl.when(ne != e)
            def _():
                for cp in weight_copies(ne, 1 - ws):
                    cp.start()
            w1b[...] = w1f[ws].astype(BF16)
            w3b[...] = w3f[ws].astype(BF16)
            w2b[...] = w2f[ws].astype(BF16)
        _on_slot(lax.rem(run, 2), load)
        run_s[0] = run + 1

    @pl.when(i < nused)
    def _():
        _on_slot(slot, gather.rows_wait)
        x = _cat([xg_s[slot, j] for j in range(n_ch)]).astype(BF16)
        a = _dot(x, w1b[...])
        b = _dot(x, w3b[...])
        hm = (a * jax.nn.sigmoid(a) * b).astype(BF16)
        y = _dot(hm, w2b[...])
        for j in range(n_ch):
            ys_ref[j] = y[:, j * LANES:(j + 1) * LANES]

    @pl.when(i >= nused)
    def _():
        ys_ref[...] = jnp.zeros(ys_ref.shape, F32)

    @pl.when(has_next2)
    def _():
        _on_slot(slot, functools.partial(gather.idx_wait, idx_nx2_ref))


def _ffn(block_exp, next_exp, n_used, slot_tok3, h2, w1, w3, w2):
    n_ch, n, _ = h2.shape
    d = n_ch * LANES
    n_blocks, _, bm = slot_tok3.shape
    de = w1.shape[-1]
    grid_spec = pltpu.PrefetchScalarGridSpec(
        num_scalar_prefetch=3,
        grid=(n_blocks,),
        in_specs=[
            pl.BlockSpec((1, 1, bm), lambda i, *_: (i, 0, 0)),
            pl.BlockSpec((1, 1, bm), lambda i, *_: (jnp.minimum(i + 1, n_blocks - 1), 0, 0)),
            pl.BlockSpec((1, 1, bm), lambda i, *_: (jnp.minimum(i + 2, n_blocks - 1), 0, 0)),
            pl.BlockSpec(memory_space=pl.ANY),
            pl.BlockSpec(memory_space=pl.ANY),
            pl.BlockSpec(memory_space=pl.ANY),
            pl.BlockSpec(memory_space=pl.ANY),
        ],
        out_specs=pl.BlockSpec((n_ch, bm, LANES), lambda i, *_: (0, i, 0)),
        scratch_shapes=[
            pltpu.SMEM((2 * bm,), I32),
            pltpu.SMEM((1,), I32),
            pltpu.VMEM((2, n_ch, bm, LANES), F32),
            pltpu.VMEM((2, d, de), F32),
            pltpu.VMEM((2, d, de), F32),
            pltpu.VMEM((2, de, d), F32),
            pltpu.VMEM((d, de), BF16),
            pltpu.VMEM((d, de), BF16),
            pltpu.VMEM((de, d), BF16),
            pltpu.SemaphoreType.DMA((2,)),
            pltpu.SemaphoreType.DMA((2,)),
            pltpu.SemaphoreType.DMA((2,)),
        ],
    )
    return pl.pallas_call(
        _ffn_kernel,
        grid_spec=grid_spec,
        out_shape=jax.ShapeDtypeStruct((n_ch, n_blocks * bm, LANES), F32),
        compiler_params=pltpu.CompilerParams(
            dimension_semantics=("arbitrary",), vmem_limit_bytes=VMEM_LIMIT),
        name="expert_ffn",
    )(block_exp, next_exp, n_used, slot_tok3, slot_tok3, slot_tok3, h2, w1, w3, w2)


def _combine_kernel(idx_cur_ref, idx_nxt_ref, x1_ref, mod_ref, gcol_ref, ys_hbm, l2g_ref, l2b_ref,
                    o_ref, idx_s, yg_s, isem, gsem):
    i = pl.program_id(0)
    nb = pl.num_programs(0)
    tc = x1_ref.shape[0]
    n_ch = yg_s.shape[1]
    slot = lax.rem(i, 2)
    nslot = 1 - slot
    gather = _RowGather(ys_hbm, idx_s, yg_s, isem, gsem, 2 * tc, priorities=(0, 1))

    @pl.when(i == 0)
    def _():
        gather.idx_start(idx_cur_ref, 0)
        gather.idx_wait(idx_cur_ref, 0)
        gather.rows_start(0)

    @pl.when(i + 1 < nb)
    def _():
        _on_slot(nslot, functools.partial(gather.idx_start, idx_nxt_ref))

    _on_slot(slot, gather.rows_wait)

    @pl.when(i + 1 < nb)
    def _():
        def nxt(s):
            gather.idx_wait(idx_nxt_ref, s)
            gather.rows_start(s)
        _on_slot(nslot, nxt)

    g2 = mod_ref[0, 5:6, :]

    def rows_of(first, r0):
        start = pl.multiple_of(first + r0, LN_ROWS)
        return _cat([yg_s[slot, j, pl.ds(start, LN_ROWS), :] for j in range(n_ch)])

    def body(j, c):
        r0 = pl.multiple_of(j * LN_ROWS, LN_ROWS)
        gc = gcol_ref[pl.ds(r0, LN_ROWS), :]
        y0 = rows_of(0, r0)
        y1 = rows_of(tc, r0)
        ffn = gc[:, 0:1] * y0 + gc[:, 1:2] * y1
        r = ALPHA * x1_ref[pl.ds(r0, LN_ROWS), :] + g2 * ffn
        o_ref[pl.ds(r0, LN_ROWS), :] = _ln(r, l2g_ref[...], l2b_ref[...])
        return c
    lax.fori_loop(0, tc // LN_ROWS, body, 0)


def _combine(dest3, x1f, mod3, gcol, ys, ln2_g, ln2_b, tokens_per_seq):
    n, d = x1f.shape
    tc = COMB_TC
    nb = n // tc
    per_seq = tokens_per_seq // tc
    return pl.pallas_call(
        _combine_kernel,
        grid=(nb,),
        in_specs=[
            pl.BlockSpec((1, 1, 2 * tc), lambda i: (i, 0, 0)),
            pl.BlockSpec((1, 1, 2 * tc), lambda i: (jnp.minimum(i + 1, nb - 1), 0, 0)),
            pl.BlockSpec((tc, d), lambda i: (i, 0)),
            pl.BlockSpec((1, 6, d), lambda i: (i // per_seq, 0, 0)),
            pl.BlockSpec((tc, LANES), lambda i: (i, 0)),
            pl.BlockSpec(memory_space=pl.ANY),
            pl.BlockSpec((1, d), lambda i: (0, 0)),
            pl.BlockSpec((1, d), lambda i: (0, 0)),
        ],
        out_specs=pl.BlockSpec((tc, d), lambda i: (i, 0)),
        out_shape=jax.ShapeDtypeStruct((n, d), F32),
        scratch_shapes=[
            pltpu.SMEM((4 * tc,), I32),
            pltpu.VMEM((2, d // LANES, 2 * tc, LANES), F32),
            pltpu.SemaphoreType.DMA((2,)),
            pltpu.SemaphoreType.DMA((2,)),
        ],
        compiler_params=pltpu.CompilerParams(
            dimension_semantics=("arbitrary",), vmem_limit_bytes=VMEM_LIMIT),
        name="combine",
    )(dest3, dest3, x1f, mod3, gcol, ys, ln2_g.reshape(1, d), ln2_b.reshape(1, d))


def _layer(x, mod3, l, w_in, conf_conv_w, conf_conv_b, conf_ln_g, conf_ln_b, ssm_conv_w, ssm_conv_b,
           ssm_dt_bias, ssm_A_log, ssm_D, ssm_norm_w, w_out, ln1_g, ln1_b, router_group_w,
           router_group_b, router_expert_w, router_expert_b, expert_w1, expert_w3, expert_w2,
           ln2_g, ln2_b):
    bsz, t, d = x.shape
    n = bsz * t
    x1 = _mixer(x, mod3, w_in[l], conf_conv_w[l], conf_conv_b[l], conf_ln_g[l], conf_ln_b[l],
                ssm_conv_w[l], ssm_conv_b[l], ssm_dt_bias[l], ssm_A_log[l], ssm_D[l], ssm_norm_w[l],
                w_out[l], ln1_g[l], ln1_b[l])
    x1f = x1.reshape(n, d)

    n_router = N_EXPERT_GROUPS + N_EXPERTS
    wr = jnp.concatenate([router_group_w[l], router_expert_w[l]], axis=1).T
    wr = jnp.pad(wr, ((0, LANES - n_router), (0, 0)))
    br = jnp.pad(jnp.concatenate([router_group_b[l], router_expert_b[l]]), (0, LANES - n_router))
    h2, eidx, rank, gcol, cnt = _route(x1f, mod3, wr, br.reshape(LANES, 1), t)

    bm = FFN_BM
    e_ids = jnp.arange(N_EXPERTS, dtype=I32)
    counts = cnt[:, 0].astype(I32)
    padded = ((counts + bm - 1) // bm) * bm
    pad_ends = jnp.cumsum(padded)
    pad_starts = pad_ends - padded
    n_slots = ((2 * n + bm - 1) // bm) * bm + N_EXPERTS * bm
    n_blocks = n_slots // bm
    dest = rank + jnp.sum(jnp.where(eidx[None] == e_ids[:, None, None], pad_starts[:, None, None], 0), axis=0)
    tok = jnp.broadcast_to(jnp.arange(n, dtype=I32), (2, n))
    slot_tok = jnp.zeros((n_slots,), I32).at[dest.reshape(-1)].set(
        tok.reshape(-1), unique_indices=True, indices_are_sorted=False, mode="promise_in_bounds")
    block_start = jnp.arange(n_blocks, dtype=I32) * bm
    block_exp = jnp.minimum(jnp.sum((pad_ends[None, :] <= block_start[:, None]).astype(I32), axis=1),
                            N_EXPERTS - 1)
    n_used = (pad_ends[-1:] // bm).astype(I32)
    later = jnp.where((padded > 0)[None, :] & (e_ids[None, :] > e_ids[:, None]), e_ids[None, :], N_EXPERTS)
    nxt = jnp.min(later, axis=1)
    next_of_expert = jnp.where(nxt < N_EXPERTS, nxt, e_ids)
    next_exp = jnp.sum(jnp.where(block_exp[:, None] == e_ids[None, :], next_of_expert[None, :], 0), axis=1)

    ys = _ffn(block_exp, next_exp.astype(I32), n_used, slot_tok.reshape(n_blocks, 1, bm), h2,
              expert_w1[l], expert_w3[l], expert_w2[l])

    tc = COMB_TC
    dest3 = dest.reshape(2, n // tc, tc).transpose(1, 0, 2).reshape(n // tc, 1, 2 * tc)
    out = _combine(dest3, x1f, mod3, gcol, ys, ln2_g[l], ln2_b[l], t)
    return out.reshape(bsz, t, d)


def kernel(x, c, w_ada, b_ada, w_in, conf_conv_w, conf_conv_b, conf_ln_g, conf_ln_b, ssm_conv_w,
           ssm_conv_b, ssm_dt_bias, ssm_A_log, ssm_D, ssm_norm_w, w_out, ln1_g, ln1_b,
           router_group_w, router_group_b, router_expert_w, router_expert_b, expert_w1, expert_w3,
           expert_w2, ln2_g, ln2_b):
    bsz, t, d = x.shape
    for l in range(w_ada.shape[0]):
        mod3 = _ada(c, w_ada[l], b_ada[l]).reshape(bsz, 6, d)
        x = _layer(x, mod3, l, w_in, conf_conv_w, conf_conv_b, conf_ln_g, conf_ln_b, ssm_conv_w,
                   ssm_conv_b, ssm_dt_bias, ssm_A_log, ssm_D, ssm_norm_w, w_out, ln1_g, ln1_b,
                   router_group_w, router_group_b, router_expert_w, router_expert_b, expert_w1,
                   expert_w3, expert_w2, ln2_g, ln2_b)
    return x
```

```python
import functools

import jax
import jax.numpy as jnp
from jax import lax
from jax.experimental import pallas as pl
from jax.experimental.pallas import tpu as pltpu

F32 = jnp.float32
BF16 = jnp.bfloat16
I32 = jnp.int32

EPS = 1e-5
LANES = 128
SUBLANES = 8
VMEM_LIMIT = 56 * 1024 * 1024

CONF_KERNEL = 31
SSM_HEADS = 16
SSM_HEAD_DIM = 64
SSM_GROUPS = 4
SSM_STATE = 128
SSM_CONV = 4
SSD_CHUNK = 128
N_EXPERT_GROUPS = 8
EXPERTS_PER_GROUP = 8
N_EXPERTS = N_EXPERT_GROUPS * EXPERTS_PER_GROUP
DEPTH = 1
ALPHA = (2.0 * DEPTH) ** 0.25

MIX_L = 256
ROWS = 32
LN_ROWS = 128
HIST_U = 32
HIST_X = 8
ROUTE_TB = 256
FFN_BM = 256
COMB_TC = 512
DMA_UNROLL = 8


def _ln(v, g, b):
    mu = jnp.mean(v, axis=-1, keepdims=True)
    d = v - mu
    var = jnp.mean(d * d, axis=-1, keepdims=True)
    return d * lax.rsqrt(var + EPS) * g + b


def _split3(a):
    hi = a.astype(BF16)
    r1 = a - hi.astype(F32)
    mid = r1.astype(BF16)
    lo = (r1 - mid.astype(F32)).astype(BF16)
    return hi, mid, lo


def _dot(a, b):
    return jnp.dot(a, b, preferred_element_type=F32)


def _cat(parts):
    return jnp.concatenate(parts, axis=1)


_NT = (((1,), (1,)), ((), ()))
_TN = (((0,), (0,)), ((), ()))


def _ada_kernel(c_ref, w_ref, b_ref, o_ref):
    c = c_ref[...]
    cond = c * jax.nn.sigmoid(c)
    c3 = _split3(cond)
    w3 = _split3(w_ref[...])
    acc = _dot(c3[0], w3[0])
    for i, j in ((0, 1), (1, 0), (0, 2), (2, 0), (1, 1)):
        acc = acc + _dot(c3[i], w3[j])
    o_ref[...] = acc + b_ref[...]


def _ada(c, w_ada, b_ada):
    bsz, d = c.shape
    n = w_ada.shape[1]
    return pl.pallas_call(
        _ada_kernel,
        grid=(n // d,),
        in_specs=[
            pl.BlockSpec((bsz, d), lambda j: (0, 0)),
            pl.BlockSpec((d, d), lambda j: (0, j)),
            pl.BlockSpec((1, d), lambda j: (0, j)),
        ],
        out_specs=pl.BlockSpec((bsz, d), lambda j: (0, j)),
        out_shape=jax.ShapeDtypeStruct((bsz, n), F32),
        compiler_params=pltpu.CompilerParams(dimension_semantics=("arbitrary",)),
        name="adaln",
    )(c, w_ada, b_ada.reshape(1, n))


def _mixer_kernel(x_ref, mod_ref, win_ref, ccw_ref, ccb_ref, clg_ref, clb_ref,
                  scw_ref, scb_ref, dtb_ref, alog_ref, dpair_ref, nw_ref, wout_ref,
                  l1g_ref, l1b_ref, o_ref,
                  h_s, cv_s, z_s, dt_s, u_s, c_s, xb_s, xc_s, y_s, mix_s, st_s):
    L = MIX_L
    d = x_ref.shape[-1]
    n_cb = d // LANES
    n_xb = xb_s.shape[0]
    n_sub = L // SSD_CHUNK

    @pl.when(pl.program_id(1) == 0)
    def _():
        u_s[:, 0:HIST_U, :] = jnp.zeros((n_cb, HIST_U, LANES), F32)
        xb_s[:, 0:HIST_X, :] = jnp.zeros((n_xb, HIST_X, LANES), F32)
        st_s[...] = jnp.zeros(st_s.shape, F32)

    sh1 = mod_ref[0, 0:1, :]
    sc1 = mod_ref[0, 1:2, :]
    g1 = mod_ref[0, 2:3, :]

    h_s[...] = (x_ref[0] * (1.0 + sc1) + sh1).astype(BF16)
    c0, c1, c2, c3 = 2 * d, 3 * d, 3 * d + n_xb * LANES, 3 * d + n_xb * LANES + LANES
    cv_s[...] = _dot(h_s[...], win_ref[:, 0:c0])
    z_s[...] = _dot(h_s[...], win_ref[:, c0:c1])
    xbc = _dot(h_s[...], win_ref[:, c1:c2])
    for cb in range(n_xb):
        xb_s[cb, HIST_X:HIST_X + L, :] = xbc[:, cb * LANES:(cb + 1) * LANES]
    dt_s[...] = _dot(h_s[...], win_ref[:, c2:c3])

    def glu(i, c):
        r0 = pl.multiple_of(i * ROWS, ROWS)
        val = cv_s[pl.ds(r0, ROWS), 0:d]
        gate = cv_s[pl.ds(r0, ROWS), d:2 * d]
        u = val * jax.nn.sigmoid(gate)
        for cb in range(n_cb):
            u_s[cb, pl.ds(pl.multiple_of(HIST_U + r0, SUBLANES), ROWS), :] = u[:, cb * LANES:(cb + 1) * LANES]
        return c
    lax.fori_loop(0, L // ROWS, glu, 0)

    def conv_u(idx, c):
        cb = idx // n_sub
        rb = idx - cb * n_sub
        r0 = pl.multiple_of(rb * SSD_CHUNK, SSD_CHUNK)
        wblk = ccw_ref[cb]
        acc = jnp.broadcast_to(ccb_ref[cb], (SSD_CHUNK, LANES))
        for k in range(CONF_KERNEL):
            off = HIST_U - (CONF_KERNEL - 1) + k
            acc = acc + wblk[k:k + 1, :] * u_s[cb, pl.ds(r0 + off, SSD_CHUNK), :]
        c_s[cb, pl.ds(r0, SSD_CHUNK), :] = acc
        return c
    lax.fori_loop(0, n_cb * n_sub, conv_u, 0)
    for cb in range(n_cb):
        u_s[cb, 0:HIST_U, :] = u_s[cb, L:L + HIST_U, :]

    def ln_u(i, c):
        r0 = pl.multiple_of(i * LN_ROWS, LN_ROWS)
        v = _cat([c_s[cb, pl.ds(r0, LN_ROWS), :] for cb in range(n_cb)])
        y = _ln(v, clg_ref[...], clb_ref[...])
        y = y * jax.nn.sigmoid(y)
        mix_s[pl.ds(r0, LN_ROWS), 0:d] = y.astype(BF16)
        return c
    lax.fori_loop(0, L // LN_ROWS, ln_u, 0)

    def conv_x(idx, c):
        cb = idx // n_sub
        rb = idx - cb * n_sub
        r0 = pl.multiple_of(rb * SSD_CHUNK, SSD_CHUNK)
        wblk = scw_ref[cb]
        acc = jnp.broadcast_to(scb_ref[cb], (SSD_CHUNK, LANES))
        for k in range(SSM_CONV):
            off = HIST_X - (SSM_CONV - 1) + k
            acc = acc + wblk[k:k + 1, :] * xb_s[cb, pl.ds(r0 + off, SSD_CHUNK), :]
        xc_s[cb, pl.ds(r0, SSD_CHUNK), :] = acc * jax.nn.sigmoid(acc)
        return c
    lax.fori_loop(0, n_xb * n_sub, conv_x, 0)
    for cb in range(n_xb):
        xb_s[cb, 0:HIST_X, :] = xb_s[cb, L:L + HIST_X, :]

    heads_per_group = SSM_HEADS // SSM_GROUPS
    row_i = lax.broadcasted_iota(I32, (SSD_CHUNK, SSD_CHUNK), 0)
    col_i = lax.broadcasted_iota(I32, (SSD_CHUNK, SSD_CHUNK), 1)
    causal = row_i >= col_i
    tri = jnp.where(causal, 1.0, 0.0).astype(BF16)
    low_half = col_i < SSM_HEAD_DIM
    a_neg = -jnp.exp(alog_ref[...])
    b_off = n_cb
    c_off = n_cb + SSM_GROUPS

    for j in range(n_sub):
        rs = slice(j * SSD_CHUNK, (j + 1) * SSD_CHUNK)
        dtr = dt_s[rs, :] + dtb_ref[...]
        dt = jnp.maximum(dtr, 0.0) + jnp.log1p(jnp.exp(-jnp.abs(dtr)))
        a3 = _split3(dt * a_neg)
        acs = _dot(tri, a3[0]) + _dot(tri, a3[1]) + _dot(tri, a3[2])
        acs_t = acs.T
        dt_t = dt.T
        last = acs[SSD_CHUNK - 1:SSD_CHUNK, :]
        e_all = jnp.exp(acs)
        w_all = dt * jnp.exp(last - acs)
        for g in range(SSM_GROUPS):
            bg = xc_s[b_off + g, rs, :].astype(BF16)
            cg = xc_s[c_off + g, rs, :].astype(BF16)
            cbm = lax.dot_general(cg, bg, _NT, preferred_element_type=F32)
            prev = st_s[g]
            yoff = _dot(cg, prev.astype(BF16))
            for half in range(heads_per_group // 2):
                pair = (g * heads_per_group) // 2 + half
                h0 = 2 * pair
                xs_pair = xc_s[pair, rs, :]
                ypair = None
                for q in range(2):
                    hh = h0 + q
                    seg = acs[:, hh:hh + 1] - acs_t[hh:hh + 1, :]
                    dec = jnp.exp(jnp.where(causal, seg, -jnp.inf))
                    m = (cbm * dec * dt_t[hh:hh + 1, :]).astype(BF16)
                    keep = low_half if q == 0 else jnp.logical_not(low_half)
                    rhs = jnp.where(keep, xs_pair, 0.0).astype(BF16)
                    part = _dot(m, rhs)
                    ypair = part if ypair is None else ypair + part
                e_pair = jnp.where(low_half, e_all[:, h0:h0 + 1], e_all[:, h0 + 1:h0 + 2])
                w_pair = jnp.where(low_half, w_all[:, h0:h0 + 1], w_all[:, h0 + 1:h0 + 2])
                cs = slice(half * LANES, (half + 1) * LANES)
                y_s[pair, rs, :] = ypair + yoff[:, cs] * e_pair + dpair_ref[pair] * xs_pair
                new = lax.dot_general(bg, (xs_pair * w_pair).astype(BF16), _TN,
                                      preferred_element_type=F32)
                st_s[g, :, cs] = e_pair[SSD_CHUNK - 1:SSD_CHUNK, :] * prev[:, cs] + new

    grp_w = d // SSM_GROUPS

    def gate_norm(i, c):
        r0 = pl.multiple_of(i * LN_ROWS, LN_ROWS)
        y = _cat([y_s[cb, pl.ds(r0, LN_ROWS), :] for cb in range(n_cb)])
        z = z_s[pl.ds(r0, LN_ROWS), :]
        gg = y * (z * jax.nn.sigmoid(z))
        outs = []
        for g in range(SSM_GROUPS):
            v = gg[:, g * grp_w:(g + 1) * grp_w]
            ms = jnp.mean(v * v, axis=-1, keepdims=True)
            outs.append(v * lax.rsqrt(ms + EPS))
        gn = _cat(outs) * nw_ref[...]
        mix_s[pl.ds(r0, LN_ROWS), d:2 * d] = gn.astype(BF16)
        return c
    lax.fori_loop(0, L // LN_ROWS, gate_norm, 0)

    cv_s[:, 0:d] = _dot(mix_s[...], wout_ref[...])

    def out_ln(i, c):
        r0 = pl.multiple_of(i * LN_ROWS, LN_ROWS)
        r = ALPHA * x_ref[0, pl.ds(r0, LN_ROWS), :] + g1 * cv_s[pl.ds(r0, LN_ROWS), 0:d]
        o_ref[0, pl.ds(r0, LN_ROWS), :] = _ln(r, l1g_ref[...], l1b_ref[...])
        return c
    lax.fori_loop(0, L // LN_ROWS, out_ln, 0)


def _const_spec(shape):
    nd = len(shape)
    return pl.BlockSpec(shape, lambda b, t: (0,) * nd)


def _mixer(x, mod3, w_in, conf_conv_w, conf_conv_b, conf_ln_g, conf_ln_b, ssm_conv_w, ssm_conv_b,
           ssm_dt_bias, ssm_A_log, ssm_D, ssm_norm_w, w_out, ln1_g, ln1_b):
    bsz, t, d = x.shape
    L = MIX_L
    xbc_dim = ssm_conv_w.shape[1]
    n_cb = d // LANES
    n_xb = xbc_dim // LANES
    in_cols = w_in.shape[1]
    win_p = jnp.pad(w_in, ((0, 0), (0, LANES - SSM_HEADS))).astype(BF16)
    ccw = jnp.pad(conf_conv_w, ((0, HIST_U - CONF_KERNEL), (0, 0))).reshape(HIST_U, n_cb, LANES).transpose(1, 0, 2)
    ccb = conf_conv_b.reshape(n_cb, 1, LANES)
    scw = jnp.pad(ssm_conv_w, ((0, SUBLANES - SSM_CONV), (0, 0))).reshape(SUBLANES, n_xb, LANES).transpose(1, 0, 2)
    scb = ssm_conv_b.reshape(n_xb, 1, LANES)
    dtb = jnp.pad(ssm_dt_bias, (0, LANES - SSM_HEADS)).reshape(1, LANES)
    alog = jnp.pad(ssm_A_log, (0, LANES - SSM_HEADS)).reshape(1, LANES)
    dpair = jnp.repeat(ssm_D, SSM_HEAD_DIM).reshape(n_cb, 1, LANES)
    row = lambda v: v.reshape(1, -1)

    in_specs = [
        pl.BlockSpec((1, L, d), lambda b, tt: (b, tt, 0)),
        pl.BlockSpec((1, 6, d), lambda b, tt: (b, 0, 0)),
        _const_spec((d, in_cols + LANES - SSM_HEADS)),
        _const_spec((n_cb, HIST_U, LANES)),
        _const_spec((n_cb, 1, LANES)),
        _const_spec((1, d)),
        _const_spec((1, d)),
        _const_spec((n_xb, SUBLANES, LANES)),
        _const_spec((n_xb, 1, LANES)),
        _const_spec((1, LANES)),
        _const_spec((1, LANES)),
        _const_spec((n_cb, 1, LANES)),
        _const_spec((1, d)),
        _const_spec((2 * d, d)),
        _const_spec((1, d)),
        _const_spec((1, d)),
    ]
    scratch = [
        pltpu.VMEM((L, d), BF16),
        pltpu.VMEM((L, 2 * d), F32),
        pltpu.VMEM((L, d), F32),
        pltpu.VMEM((L, LANES), F32),
        pltpu.VMEM((n_cb, HIST_U + L, LANES), F32),
        pltpu.VMEM((n_cb, L, LANES), F32),
        pltpu.VMEM((n_xb, HIST_X + L, LANES), F32),
        pltpu.VMEM((n_xb, L, LANES), F32),
        pltpu.VMEM((n_cb, L, LANES), F32),
        pltpu.VMEM((L, 2 * d), BF16),
        pltpu.VMEM((SSM_GROUPS, SSM_STATE, d // SSM_GROUPS), F32),
    ]
    return pl.pallas_call(
        _mixer_kernel,
        grid=(bsz, t // L),
        in_specs=in_specs,
        out_specs=pl.BlockSpec((1, L, d), lambda b, tt: (b, tt, 0)),
        out_shape=jax.ShapeDtypeStruct((bsz, t, d), F32),
        scratch_shapes=scratch,
        compiler_params=pltpu.CompilerParams(
            dimension_semantics=("arbitrary", "arbitrary"), vmem_limit_bytes=VMEM_LIMIT),
        name="mixer",
    )(x, mod3, win_p, ccw, ccb, row(conf_ln_g), row(conf_ln_b), scw, scb, dtb, alog, dpair,
      row(ssm_norm_w), w_out.astype(BF16), row(ln1_g), row(ln1_b))


def _route_kernel(x1_ref, mod_ref, wr_ref, br_ref, h2_ref, eidx_ref, rank_ref, gcol_ref, cnt_ref,
                  carry_s):
    tb = x1_ref.shape[0]

    @pl.when(pl.program_id(0) == 0)
    def _():
        carry_s[...] = jnp.zeros(carry_s.shape, F32)

    sh2 = mod_ref[0, 3:4, :]
    sc2 = mod_ref[0, 4:5, :]
    h2 = x1_ref[...] * (1.0 + sc2) + sh2
    for j in range(h2_ref.shape[0]):
        h2_ref[j] = h2[:, j * LANES:(j + 1) * LANES]

    w3 = _split3(wr_ref[...])
    h3 = _split3(h2)
    lg = None
    for i, j in ((0, 0), (0, 1), (1, 0), (0, 2), (2, 0), (1, 1)):
        part = lax.dot_general(w3[i], h3[j], _NT, preferred_element_type=F32)
        lg = part if lg is None else lg + part
    lg = lg + br_ref[...]

    epg = EXPERTS_PER_GROUP
    io8 = lax.broadcasted_iota(I32, (epg, tb), 0).astype(F32)

    def softmax0(v):
        e = jnp.exp(v - jnp.max(v, axis=0, keepdims=True))
        return e / jnp.sum(e, axis=0, keepdims=True)

    def top1(p):
        pm = jnp.max(p, axis=0, keepdims=True)
        idx = jnp.min(jnp.where(p == pm, io8, float(epg)), axis=0, keepdims=True)
        return pm, idx

    pg, gidx = top1(softmax0(lg[0:N_EXPERT_GROUPS, :]))
    sel = jnp.zeros((epg, tb), F32)
    for g in range(N_EXPERT_GROUPS):
        lo = N_EXPERT_GROUPS + g * epg
        sel = sel + jnp.where(gidx == float(g), lg[lo:lo + epg, :], 0.0)
    pe = softmax0(sel)
    p1, i1 = top1(pe)
    p2, i2 = top1(jnp.where(io8 == i1, -1.0, pe))
    den = p1 + p2
    gate0 = pg * (p1 / den)
    gate1 = pg * (p2 / den)
    e0 = (gidx * float(epg) + i1).astype(I32)
    e1 = (gidx * float(epg) + i2).astype(I32)

    io_e = lax.broadcasted_iota(I32, (N_EXPERTS, tb), 0)
    oh0 = jnp.where(io_e == e0, 1.0, 0.0)
    oh1 = jnp.where(io_e == e1, 1.0, 0.0)
    oh = oh0 + oh1
    rr = lax.broadcasted_iota(I32, (tb, tb), 0)
    cc = lax.broadcasted_iota(I32, (tb, tb), 1)
    before = jnp.where(rr < cc, 1.0, 0.0).astype(BF16)
    base = _dot(oh.astype(BF16), before) + carry_s[...]
    r0 = jnp.sum(oh0 * base, axis=0, keepdims=True)
    r1 = jnp.sum(oh1 * base, axis=0, keepdims=True)
    carry_s[...] = carry_s[...] + jnp.sum(oh, axis=1, keepdims=True)
    cnt_ref[...] = carry_s[...]

    eidx_ref[0:1, :] = e0
    eidx_ref[1:2, :] = e1
    rank_ref[0:1, :] = r0.astype(I32)
    rank_ref[1:2, :] = r1.astype(I32)
    io_l = lax.broadcasted_iota(I32, (LANES, tb), 0)
    gt = jnp.where(io_l == 0, gate0, jnp.where(io_l == 1, gate1, 0.0))
    gcol_ref[...] = gt.T


def _route(x1f, mod3, wr, br, tokens_per_seq):
    n, d = x1f.shape
    tb = ROUTE_TB
    per_seq = tokens_per_seq // tb
    return pl.pallas_call(
        _route_kernel,
        grid=(n // tb,),
        in_specs=[
            pl.BlockSpec((tb, d), lambda i: (i, 0)),
            pl.BlockSpec((1, 6, d), lambda i: (i // per_seq, 0, 0)),
            pl.BlockSpec((LANES, d), lambda i: (0, 0)),
            pl.BlockSpec((LANES, 1), lambda i: (0, 0)),
        ],
        out_specs=[
            pl.BlockSpec((d // LANES, tb, LANES), lambda i: (0, i, 0)),
            pl.BlockSpec((2, tb), lambda i: (0, i)),
            pl.BlockSpec((2, tb), lambda i: (0, i)),
            pl.BlockSpec((tb, LANES), lambda i: (i, 0)),
            pl.BlockSpec((N_EXPERTS, tb), lambda i: (0, 0)),
        ],
        out_shape=[
            jax.ShapeDtypeStruct((d // LANES, n, LANES), F32),
            jax.ShapeDtypeStruct((2, n), I32),
            jax.ShapeDtypeStruct((2, n), I32),
            jax.ShapeDtypeStruct((n, LANES), F32),
            jax.ShapeDtypeStruct((N_EXPERTS, tb), F32),
        ],
        scratch_shapes=[pltpu.VMEM((N_EXPERTS, tb), F32)],
        compiler_params=pltpu.CompilerParams(dimension_semantics=("arbitrary",)),
        name="router",
    )(x1f, mod3, wr, br)


def _on_slot(slot, fn):
    for s in range(2):
        @pl.when(slot == s)
        def _():
            fn(s)


class _RowGather:
    def __init__(self, src_hbm, idx_s, buf_s, isem, gsem, n_rows, priorities):
        self.src, self.idx_s, self.buf, self.isem, self.gsem, self.n = src_hbm, idx_s, buf_s, isem, gsem, n_rows
        self.priorities = priorities

    def _idx_copy(self, idx_vmem_ref, s):
        return pltpu.make_async_copy(idx_vmem_ref.at[0, 0], self.idx_s.at[pl.ds(s * self.n, self.n)],
                                     self.isem.at[s])

    def idx_start(self, idx_vmem_ref, s):
        self._idx_copy(idx_vmem_ref, s).start()

    def idx_wait(self, idx_vmem_ref, s):
        self._idx_copy(idx_vmem_ref, s).wait()

    def rows_start(self, s):
        def body(r8, c):
            for u in range(DMA_UNROLL):
                r = r8 * DMA_UNROLL + u
                row = self.idx_s[s * self.n + r]
                pltpu.make_async_copy(self.src.at[:, row], self.buf.at[s, :, r], self.gsem.at[s]).start(
                    priority=self.priorities[u % len(self.priorities)])
            return c
        lax.fori_loop(0, self.n // DMA_UNROLL, body, 0)

    def rows_wait(self, s):
        pltpu.make_async_copy(self.src.at[:, pl.ds(0, self.n)], self.buf.at[s], self.gsem.at[s]).wait()


def _ffn_kernel(bexp_ref, nexp_ref, nused_ref, idx_cur_ref, idx_nxt_ref, idx_nx2_ref, h2_hbm,
                w1_hbm, w3_hbm, w2_hbm,
                ys_ref, idx_s, run_s, xg_s, w1f, w3f, w2f, w1b, w3b, w2b, isem, gsem, wsem):
    i = pl.program_id(0)
    nb = pl.num_programs(0)
    nused = nused_ref[0]
    n_ch, bm = ys_ref.shape[0], ys_ref.shape[1]
    slot = lax.rem(i, 2)
    nslot = 1 - slot
    gather = _RowGather(h2_hbm, idx_s, xg_s, isem, gsem, bm, priorities=(1,))

    def weight_copies(e, s):
        return [pltpu.make_async_copy(src.at[e], dst.at[s], wsem.at[s])
                for src, dst in ((w1_hbm, w1f), (w3_hbm, w3f), (w2_hbm, w2f))]

    @pl.when(i == 0)
    def _():
        run_s[0] = 0
        gather.idx_start(idx_cur_ref, 0)
        gather.idx_wait(idx_cur_ref, 0)

        @pl.when(nused > 0)
        def _():
            for cp in weight_copies(bexp_ref[0], 0):
                cp.start()
            gather.rows_start(0)

        @pl.when(jnp.logical_and(1 < nb, 1 < nused))
        def _():
            gather.idx_start(idx_nxt_ref, 1)
            gather.idx_wait(idx_nxt_ref, 1)

    has_next = jnp.logical_and(i + 1 < nb, i + 1 < nused)
    has_next2 = jnp.logical_and(i + 2 < nb, i + 2 < nused)

    @pl.when(has_next)
    def _():
        _on_slot(nslot, gather.rows_start)

    @pl.when(has_next2)
    def _():
        _on_slot(slot, functools.partial(gather.idx_start, idx_nx2_ref))

    first_of_run = jnp.logical_or(i == 0, bexp_ref[i] != bexp_ref[jnp.maximum(i - 1, 0)])

    @pl.when(jnp.logical_and(first_of_run, i < nused))
    def _():
        e = bexp_ref[i]
        ne = nexp_ref[i]
        run = run_s[0]

        def load(ws):
            for cp in weight_copies(e, ws):
                cp.wait()

            @pl.when(ne != e)
            def _():
                for cp in weight_copies(ne, 1 - ws):
                    cp.start()
            w1b[...] = w1f[ws].astype(BF16)
            w3b[...] = w3f[ws].astype(BF16)
            w2b[...] = w2f[ws].astype(BF16)
        _on_slot(lax.rem(run, 2), load)
        run_s[0] = run + 1

    @pl.when(i < nused)
    def _():
        _on_slot(slot, gather.rows_wait)
        x = _cat([xg_s[slot, j] for j in range(n_ch)]).astype(BF16)
        a = _dot(x, w1b[...])
        b = _dot(x, w3b[...])
        hm = (a * jax.nn.sigmoid(a) * b).astype(BF16)
        y = _dot(hm, w2b[...])
        for j in range(n_ch):
            ys_ref[j] = y[:, j * LANES:(j + 1) * LANES]

    @pl.when(i >= nused)
    def _():
        ys_ref[...] = jnp.zeros(ys_ref.shape, F32)

    @pl.when(has_next2)
    def _():
        _on_slot(slot, functools.partial(gather.idx_wait, idx_nx2_ref))


def _ffn(block_exp, next_exp, n_used, slot_tok3, h2, w1, w3, w2):
    n_ch, n, _ = h2.shape
    d = n_ch * LANES
    n_blocks, _, bm = slot_tok3.shape
    de = w1.shape[-1]
    grid_spec = pltpu.PrefetchScalarGridSpec(
        num_scalar_prefetch=3,
        grid=(n_blocks,),
        in_specs=[
            pl.BlockSpec((1, 1, bm), lambda i, *_: (i, 0, 0)),
            pl.BlockSpec((1, 1, bm), lambda i, *_: (jnp.minimum(i + 1, n_blocks - 1), 0, 0)),
            pl.BlockSpec((1, 1, bm), lambda i, *_: (jnp.minimum(i + 2, n_blocks - 1), 0, 0)),
            pl.BlockSpec(memory_space=pl.ANY),
            pl.BlockSpec(memory_space=pl.ANY),
            pl.BlockSpec(memory_space=pl.ANY),
            pl.BlockSpec(memory_space=pl.ANY),
        ],
        out_specs=pl.BlockSpec((n_ch, bm, LANES), lambda i, *_: (0, i, 0)),
        scratch_shapes=[
            pltpu.SMEM((2 * bm,), I32),
            pltpu.SMEM((1,), I32),
            pltpu.VMEM((2, n_ch, bm, LANES), F32),
            pltpu.VMEM((2, d, de), F32),
            pltpu.VMEM((2, d, de), F32),
            pltpu.VMEM((2, de, d), F32),
            pltpu.VMEM((d, de), BF16),
            pltpu.VMEM((d, de), BF16),
            pltpu.VMEM((de, d), BF16),
            pltpu.SemaphoreType.DMA((2,)),
            pltpu.SemaphoreType.DMA((2,)),
            pltpu.SemaphoreType.DMA((2,)),
        ],
    )
    return pl.pallas_call(
        _ffn_kernel,
        grid_spec=grid_spec,
        out_shape=jax.ShapeDtypeStruct((n_ch, n_blocks * bm, LANES), F32),
        compiler_params=pltpu.CompilerParams(
            dimension_semantics=("arbitrary",), vmem_limit_bytes=VMEM_LIMIT),
        name="expert_ffn",
    )(block_exp, next_exp, n_used, slot_tok3, slot_tok3, slot_tok3, h2, w1, w3, w2)


def _combine_kernel(idx_cur_ref, idx_nxt_ref, x1_ref, mod_ref, gcol_ref, ys_hbm, l2g_ref, l2b_ref,
                    o_ref, idx_s, yg_s, isem, gsem):
    i = pl.program_id(0)
    nb = pl.num_programs(0)
    tc = x1_ref.shape[0]
    n_ch = yg_s.shape[1]
    slot = lax.rem(i, 2)
    nslot = 1 - slot
    gather = _RowGather(ys_hbm, idx_s, yg_s, isem, gsem, 2 * tc, priorities=(0, 1))

    @pl.when(i == 0)
    def _():
        gather.idx_start(idx_cur_ref, 0)
        gather.idx_wait(idx_cur_ref, 0)
        gather.rows_start(0)

    @pl.when(i + 1 < nb)
    def _():
        _on_slot(nslot, functools.partial(gather.idx_start, idx_nxt_ref))

    _on_slot(slot, gather.rows_wait)

    @pl.when(i + 1 < nb)
    def _():
        def nxt(s):
            gather.idx_wait(idx_nxt_ref, s)
            gather.rows_start(s)
        _on_slot(nslot, nxt)

    g2 = mod_ref[0, 5:6, :]

    def rows_of(first, r0):
        start = pl.multiple_of(first + r0, LN_ROWS)
        return _cat([yg_s[slot, j, pl.ds(start, LN_ROWS), :] for j in range(n_ch)])

    def body(j, c):
        r0 = pl.multiple_of(j * LN_ROWS, LN_ROWS)
        gc = gcol_ref[pl.ds(r0, LN_ROWS), :]
        y0 = rows_of(0, r0)
        y1 = rows_of(tc, r0)
        ffn = gc[:, 0:1] * y0 + gc[:, 1:2] * y1
        r = ALPHA * x1_ref[pl.ds(r0, LN_ROWS), :] + g2 * ffn
        o_ref[pl.ds(r0, LN_ROWS), :] = _ln(r, l2g_ref[...], l2b_ref[...])
        return c
    lax.fori_loop(0, tc // LN_ROWS, body, 0)


def _combine(dest3, x1f, mod3, gcol, ys, ln2_g, ln2_b, tokens_per_seq):
    n, d = x1f.shape
    tc = COMB_TC
    nb = n // tc
    per_seq = tokens_per_seq // tc
    return pl.pallas_call(
        _combine_kernel,
        grid=(nb,),
        in_specs=[
            pl.BlockSpec((1, 1, 2 * tc), lambda i: (i, 0, 0)),
            pl.BlockSpec((1, 1, 2 * tc), lambda i: (jnp.minimum(i + 1, nb - 1), 0, 0)),
            pl.BlockSpec((tc, d), lambda i: (i, 0)),
            pl.BlockSpec((1, 6, d), lambda i: (i // per_seq, 0, 0)),
            pl.BlockSpec((tc, LANES), lambda i: (i, 0)),
            pl.BlockSpec(memory_space=pl.ANY),
            pl.BlockSpec((1, d), lambda i: (0, 0)),
            pl.BlockSpec((1, d), lambda i: (0, 0)),
        ],
        out_specs=pl.BlockSpec((tc, d), lambda i: (i, 0)),
        out_shape=jax.ShapeDtypeStruct((n, d), F32),
        scratch_shapes=[
            pltpu.SMEM((4 * tc,), I32),
            pltpu.VMEM((2, d // LANES, 2 * tc, LANES), F32),
            pltpu.SemaphoreType.DMA((2,)),
            pltpu.SemaphoreType.DMA((2,)),
        ],
        compiler_params=pltpu.CompilerParams(
            dimension_semantics=("arbitrary",), vmem_limit_bytes=VMEM_LIMIT),
        name="combine",
    )(dest3, dest3, x1f, mod3, gcol, ys, ln2_g.reshape(1, d), ln2_b.reshape(1, d))


def _layer(x, mod3, l, w_in, conf_conv_w, conf_conv_b, conf_ln_g, conf_ln_b, ssm_conv_w, ssm_conv_b,
           ssm_dt_bias, ssm_A_log, ssm_D, ssm_norm_w, w_out, ln1_g, ln1_b, router_group_w,
           router_group_b, router_expert_w, router_expert_b, expert_w1, expert_w3, expert_w2,
           ln2_g, ln2_b):
    bsz, t, d = x.shape
    n = bsz * t
    x1 = _mixer(x, mod3, w_in[l], conf_conv_w[l], conf_conv_b[l], conf_ln_g[l], conf_ln_b[l],
                ssm_conv_w[l], ssm_conv_b[l], ssm_dt_bias[l], ssm_A_log[l], ssm_D[l], ssm_norm_w[l],
                w_out[l], ln1_g[l], ln1_b[l])
    x1f = x1.reshape(n, d)

    n_router = N_EXPERT_GROUPS + N_EXPERTS
    wr = jnp.concatenate([router_group_w[l], router_expert_w[l]], axis=1).T
    wr = jnp.pad(wr, ((0, LANES - n_router), (0, 0)))
    br = jnp.pad(jnp.concatenate([router_group_b[l], router_expert_b[l]]), (0, LANES - n_router))
    h2, eidx, rank, gcol, cnt = _route(x1f, mod3, wr, br.reshape(LANES, 1), t)

    bm = FFN_BM
    e_ids = jnp.arange(N_EXPERTS, dtype=I32)
    counts = cnt[:, 0].astype(I32)
    padded = ((counts + bm - 1) // bm) * bm
    pad_ends = jnp.cumsum(padded)
    pad_starts = pad_ends - padded
    n_slots = ((2 * n + bm - 1) // bm) * bm + N_EXPERTS * bm
    n_blocks = n_slots // bm
    dest = rank + jnp.sum(jnp.where(eidx[None] == e_ids[:, None, None], pad_starts[:, None, None], 0), axis=0)
    tok = jnp.broadcast_to(jnp.arange(n, dtype=I32), (2, n))
    slot_tok = jnp.zeros((n_slots,), I32).at[dest.reshape(-1)].set(
        tok.reshape(-1), unique_indices=True, indices_are_sorted=False, mode="promise_in_bounds")
    block_start = jnp.arange(n_blocks, dtype=I32) * bm
    block_exp = jnp.minimum(jnp.sum((pad_ends[None, :] <= block_start[:, None]).astype(I32), axis=1),
                            N_EXPERTS - 1)
    n_used = (pad_ends[-1:] // bm).astype(I32)
    later = jnp.where((padded > 0)[None, :] & (e_ids[None, :] > e_ids[:, None]), e_ids[None, :], N_EXPERTS)
    nxt = jnp.min(later, axis=1)
    next_of_expert = jnp.where(nxt < N_EXPERTS, nxt, e_ids)
    next_exp = jnp.sum(jnp.where(block_exp[:, None] == e_ids[None, :], next_of_expert[None, :], 0), axis=1)

    ys = _ffn(block_exp, next_exp.astype(I32), n_used, slot_tok.reshape(n_blocks, 1, bm), h2,
              expert_w1[l], expert_w3[l], expert_w2[l])

    tc = COMB_TC
    dest3 = dest.reshape(2, n // tc, tc).transpose(1, 0, 2).reshape(n // tc, 1, 2 * tc)
    out = _combine(dest3, x1f, mod3, gcol, ys, ln2_g[l], ln2_b[l], t)
    return out.reshape(bsz, t, d)


def kernel(x, c, w_ada, b_ada, w_in, conf_conv_w, conf_conv_b, conf_ln_g, conf_ln_b, ssm_conv_w,
           ssm_conv_b, ssm_dt_bias, ssm_A_log, ssm_D, ssm_norm_w, w_out, ln1_g, ln1_b,
           router_group_w, router_group_b, router_expert_w, router_expert_b, expert_w1, expert_w3,
           expert_w2, ln2_g, ln2_b):
    bsz, t, d = x.shape
    for l in range(w_ada.shape[0]):
        mod3 = _ada(c, w_ada[l], b_ada[l]).reshape(bsz, 6, d)
        x = _layer(x, mod3, l, w_in, conf_conv_w, conf_conv_b, conf_ln_g, conf_ln_b, ssm_conv_w,
                   ssm_conv_b, ssm_dt_bias, ssm_A_log, ssm_D, ssm_norm_w, w_out, ln1_g, ln1_b,
                   router_group_w, router_group_b, router_expert_w, router_expert_b, expert_w1,
                   expert_w3, expert_w2, ln2_g, ln2_b)
    return x
```
